```python
import jax, jax.numpy as jnp
from jax import lax
import numpy as np

D_MODEL = 1024
BATCH = 32
SEQ = 256
DEPTH = 4
DEC_BATCH = 4
DEC_SEQ = 4096
PAST_LEN = 256

GRID_W = 64
N_MIXERS = 2
N_DA = (DEPTH + 1) // 2
N_MLA = DEPTH // 2
DA_HEADS = 8
DA_HEAD_DIM = 64
DA_V_DIM = 2 * DA_HEAD_DIM
MLA_HEADS = 16
MLA_Q_LORA = 512
MLA_KV_LORA = 256
MLA_NOPE = 64
MLA_ROPE = 32
MLA_QK_DIM = MLA_NOPE + MLA_ROPE
MLA_V_DIM = 64
PEER_HEADS = 8
PEER_N_KEYS = 128
PEER_N_EXPERTS = PEER_N_KEYS * PEER_N_KEYS
PEER_KEY_DIM = 256
PEER_HALF = PEER_KEY_DIM // 2
PEER_TOPK = 16
Q_BLOCK = 128
TOK_BLOCK = 128
ROPE_BASE = 10000.0
EPS = 1e-6

kernel_name = "hybrid_diffattn_mla_peer_prefix_dit_step"


def rms_norm(x, g):
    xf = x.astype(jnp.float32)
    y = xf * lax.rsqrt(jnp.mean(xf * xf, axis=-1, keepdims=True) + EPS)
    return (y * g.astype(jnp.float32)).astype(x.dtype)


def rope_1d(x, pos):
    d = x.shape[-1]
    inv = 1.0 / (ROPE_BASE ** (jnp.arange(0, d, 2, dtype=jnp.float32) / d))
    ang = pos[:, None] * inv[None, :]
    cos = jnp.cos(ang)[None, :, None, :]
    sin = jnp.sin(ang)[None, :, None, :]
    xf = x.astype(jnp.float32)
    x1, x2 = xf[..., : d // 2], xf[..., d // 2:]
    return jnp.concatenate([x1 * cos - x2 * sin, x2 * cos + x1 * sin], axis=-1).astype(x.dtype)


def rope_2d(x, row, col):
    h = x.shape[-1] // 2
    return jnp.concatenate([rope_1d(x[..., :h], row), rope_1d(x[..., h:], col)], axis=-1)


def grid_positions(n):
    rows = n // GRID_W
    row = jnp.repeat(jnp.arange(rows, dtype=jnp.float32), GRID_W)
    col = jnp.tile(jnp.arange(GRID_W, dtype=jnp.float32), rows)
    return row, col


def modulation(cvec, w, b):
    m = jax.nn.silu(cvec) @ w + b
    return [t[:, None, :] for t in jnp.split(m, 6, axis=-1)]


def modulate(x, g, shift, scale):
    return rms_norm(x, g) * (1.0 + scale) + shift


def attend(q, k, v, mix):
    B, S, Hq, Dq = q.shape
    scale = Dq ** -0.5
    nb = S // Q_BLOCK
    qb = q.reshape(B, nb, Q_BLOCK, Hq, Dq).swapaxes(0, 1)

    def one(qblk):
        s = jnp.einsum('bqhd,bkhd->bhqk', qblk, k, preferred_element_type=jnp.float32) * scale
        w = mix(jax.nn.softmax(s, axis=-1))
        return jnp.einsum('bhqk,bkhd->bqhd', w.astype(v.dtype), v)

    o = lax.map(one, qb)
    return o.swapaxes(0, 1).reshape(B, S, o.shape[-2], o.shape[-1])


def da_qkv(h, wq, wk, wv, qn, kn, pos):
    B, S, _ = h.shape
    q = rms_norm((h @ wq).reshape(B, S, 2 * DA_HEADS, DA_HEAD_DIM), qn)
    k = rms_norm((h @ wk).reshape(B, S, 2 * DA_HEADS, DA_HEAD_DIM), kn)
    v = (h @ wv).reshape(B, S, DA_HEADS, DA_V_DIM)
    if pos is not None:
        q = rope_2d(q, pos[0], pos[1])
        k = rope_2d(k, pos[0], pos[1])
    return q, k, v


def da_out(q, k, v, lam, lam_init, subln, wo):
    def mix(p):
        b, h2, nq, nk = p.shape
        p = p.reshape(b, h2 // 2, 2, nq, nk)
        return p[:, :, 0] - lam * p[:, :, 1]

    o = attend(q, k, v, mix)
    o = rms_norm(o, subln) * (1.0 - lam_init)
    B, S = o.shape[:2]
    return o.reshape(B, S, DA_HEADS * DA_V_DIM) @ wo


def mla_query(h, wdq, qa_norm, wuq, q_norm, pos):
    B, S, _ = h.shape
    q = (rms_norm(h @ wdq, qa_norm) @ wuq).reshape(B, S, MLA_HEADS, MLA_QK_DIM)
    q = rms_norm(q, q_norm)
    if pos is not None:
        q = jnp.concatenate([q[..., :MLA_NOPE], rope_2d(q[..., MLA_NOPE:], pos[0], pos[1])], axis=-1)
    return q


def mla_compress(h, wdkv, kv_norm):
    ckv = h @ wdkv
    return rms_norm(ckv[..., :MLA_KV_LORA], kv_norm), ckv[..., MLA_KV_LORA:]


def mla_expand(ckv, kpe, wukv, k_norm, pos):
    B, S, _ = ckv.shape
    kv = (ckv @ wukv).reshape(B, S, MLA_HEADS, MLA_NOPE + MLA_V_DIM)
    k_nope, v = kv[..., :MLA_NOPE], kv[..., MLA_NOPE:]
    k_pe = jnp.broadcast_to(kpe[:, :, None, :], (B, S, MLA_HEADS, MLA_ROPE))
    k = rms_norm(jnp.concatenate([k_nope, k_pe], axis=-1), k_norm)
    if pos is not None:
        k = jnp.concatenate([k[..., :MLA_NOPE], rope_2d(k[..., MLA_NOPE:], pos[0], pos[1])], axis=-1)
    return k, v


def mla_out(q, k, v, wo):
    o = attend(q, k, v, lambda p: p)
    B, S = o.shape[:2]
    return o.reshape(B, S, MLA_HEADS * MLA_V_DIM) @ wo


def peer(h, wq, sub_keys, u, v):
    B, S, D = h.shape
    xb_all = h.reshape((B * S) // TOK_BLOCK, TOK_BLOCK, D)

    def one(xb):
        q = (xb @ wq).reshape(TOK_BLOCK, PEER_HEADS, 2, PEER_HALF)
        s = jnp.einsum('thcd,hcnd->thcn', q, sub_keys, preferred_element_type=jnp.float32)
        top_s, top_i = lax.top_k(s, PEER_TOPK)
        cand_s = top_s[:, :, 0, :, None] + top_s[:, :, 1, None, :]
        cand_i = top_i[:, :, 0, :, None] * PEER_N_KEYS + top_i[:, :, 1, None, :]
        cand_s = cand_s.reshape(TOK_BLOCK, PEER_HEADS, PEER_TOPK * PEER_TOPK)
        cand_i = cand_i.reshape(TOK_BLOCK, PEER_HEADS, PEER_TOPK * PEER_TOPK)
        fs, fi = lax.top_k(cand_s, PEER_TOPK)
        idx = jnp.take_along_axis(cand_i, fi, axis=-1)
        g = jax.nn.softmax(fs, axis=-1)
        ue = u[idx]
        a = jax.nn.gelu(jnp.einsum('td,thkd->thk', xb, ue, preferred_element_type=jnp.float32))
        ve = v[idx]
        return jnp.einsum('thk,thkd->td', (g * a).astype(xb.dtype), ve)

    return lax.map(one, xb_all).reshape(B, S, D)


def setup_inputs(seed: int = 0) -> dict:
    key = jax.random.key(seed)
    ks = iter(jax.random.split(key, 64))

    def nrm(shape, scale=1.0):
        return jax.random.normal(next(ks), shape, jnp.float32) * scale

    def gain(shape):
        return 1.0 + nrm(shape, 0.02)

    D = D_MODEL
    fan = D ** -0.5
    return {
        "x_prompt": nrm((BATCH, SEQ, D)),
        "x_sample": nrm((DEC_BATCH, DEC_SEQ, D)),
        "cache_da_k": nrm((DEC_BATCH, N_DA, PAST_LEN, 2 * DA_HEADS, DA_HEAD_DIM)),
        "cache_da_v": nrm((DEC_BATCH, N_DA, PAST_LEN, DA_HEADS, DA_V_DIM)),
        "cache_mla_ckv": nrm((DEC_BATCH, N_MLA, PAST_LEN, MLA_KV_LORA)),
        "cache_mla_kpe": nrm((DEC_BATCH, N_MLA, PAST_LEN, MLA_ROPE)),
        "c": nrm((DEC_BATCH, D)),
        "c_ctx": nrm((D,)),
        "norm1": gain((DEPTH, D)),
        "norm2": gain((DEPTH, D)),
        "ada_w": nrm((DEPTH, D, 6 * D), 0.5 * fan),
        "ada_b": nrm((DEPTH, 6 * D), 0.02),
        "da_wq": nrm((N_DA, D, 2 * DA_HEADS * DA_HEAD_DIM), fan),
        "da_wk": nrm((N_DA, D, 2 * DA_HEADS * DA_HEAD_DIM), fan),
        "da_wv": nrm((N_DA, D, DA_HEADS * DA_V_DIM), fan),
        "da_wo": nrm((N_DA, DA_HEADS * DA_V_DIM, D), (DA_HEADS * DA_V_DIM) ** -0.5),
        "da_q_norm": gain((N_DA, DA_HEAD_DIM)),
        "da_k_norm": gain((N_DA, DA_HEAD_DIM)),
        "da_lq1": nrm((N_DA, DA_HEAD_DIM), 0.1),
        "da_lk1": nrm((N_DA, DA_HEAD_DIM), 0.1),
        "da_lq2": nrm((N_DA, DA_HEAD_DIM), 0.1),
        "da_lk2": nrm((N_DA, DA_HEAD_DIM), 0.1),
        "da_subln": gain((N_DA, DA_V_DIM)),
        "mla_wdq": nrm((N_MLA, D, MLA_Q_LORA), fan),
        "mla_qa_norm": gain((N_MLA, MLA_Q_LORA)),
        "mla_wuq": nrm((N_MLA, MLA_Q_LORA, MLA_HEADS * MLA_QK_DIM), MLA_Q_LORA ** -0.5),
        "mla_wdkv": nrm((N_MLA, D, MLA_KV_LORA + MLA_ROPE), fan),
        "mla_kv_norm": gain((N_MLA, MLA_KV_LORA)),
        "mla_wukv": nrm((N_MLA, MLA_KV_LORA, MLA_HEADS * (MLA_NOPE + MLA_V_DIM)), MLA_KV_LORA ** -0.5),
        "mla_q_norm": gain((N_MLA, MLA_QK_DIM)),
        "mla_k_norm": gain((N_MLA, MLA_QK_DIM)),
        "mla_wo": nrm((N_MLA, MLA_HEADS * MLA_V_DIM, D), (MLA_HEADS * MLA_V_DIM) ** -0.5),
        "peer_wq": nrm((DEPTH, D, PEER_HEADS * PEER_KEY_DIM), fan),
        "peer_keys": nrm((DEPTH, PEER_HEADS, 2, PEER_N_KEYS, PEER_HALF), PEER_HALF ** -0.5),
        "peer_u": nrm((DEPTH, PEER_N_EXPERTS, D), fan),
        "peer_v": nrm((DEPTH, PEER_N_EXPERTS, D), 0.5),
    }


def reference(x_prompt, x_sample, cache_da_k, cache_da_v, cache_mla_ckv, cache_mla_kpe, c, c_ctx,
              norm1, norm2, ada_w, ada_b,
              da_wq, da_wk, da_wv, da_wo, da_q_norm, da_k_norm, da_lq1, da_lk1, da_lq2, da_lk2, da_subln,
              mla_wdq, mla_qa_norm, mla_wuq, mla_wdkv, mla_kv_norm, mla_wukv, mla_q_norm, mla_k_norm, mla_wo,
              peer_wq, peer_keys, peer_u, peer_v):
    xp = x_prompt
    xs = x_sample
    pos = grid_positions(xs.shape[1])
    new_da_k, new_da_v, new_ckv, new_kpe = [], [], [], []

    for l in range(DEPTH):
        p_sh1, p_sc1, p_g1, p_sh2, p_sc2, p_g2 = modulation(c_ctx[None, :], ada_w[l], ada_b[l])
        s_sh1, s_sc1, s_g1, s_sh2, s_sc2, s_g2 = modulation(c, ada_w[l], ada_b[l])
        hp = modulate(xp, norm1[l], p_sh1, p_sc1)
        hs = modulate(xs, norm1[l], s_sh1, s_sc1)
        j = l // N_MIXERS
        if l % N_MIXERS == 0:
            lam_init = 0.8 - 0.6 * float(np.exp(-0.3 * l))
            f32 = jnp.float32
            lam = (jnp.exp(jnp.sum(da_lq1[j].astype(f32) * da_lk1[j].astype(f32)))
                   - jnp.exp(jnp.sum(da_lq2[j].astype(f32) * da_lk2[j].astype(f32))) + lam_init)
            qp, kp, vp = da_qkv(hp, da_wq[j], da_wk[j], da_wv[j], da_q_norm[j], da_k_norm[j], None)
            new_da_k.append(kp)
            new_da_v.append(vp)
            op = da_out(qp, kp, vp, lam, lam_init, da_subln[j], da_wo[j])
            qs, ks_lat, vs_lat = da_qkv(hs, da_wq[j], da_wk[j], da_wv[j], da_q_norm[j], da_k_norm[j], pos)
            k_all = jnp.concatenate([ks_lat, cache_da_k[:, j].astype(ks_lat.dtype)], axis=1)
            v_all = jnp.concatenate([vs_lat, cache_da_v[:, j].astype(vs_lat.dtype)], axis=1)
            os_ = da_out(qs, k_all, v_all, lam, lam_init, da_subln[j], da_wo[j])
        else:
            qp = mla_query(hp, mla_wdq[j], mla_qa_norm[j], mla_wuq[j], mla_q_norm[j], None)
            ckv_p, kpe_p = mla_compress(hp, mla_wdkv[j], mla_kv_norm[j])
            new_ckv.append(ckv_p)
            new_kpe.append(kpe_p)
            kp, vp = mla_expand(ckv_p, kpe_p, mla_wukv[j], mla_k_norm[j], None)
            op = mla_out(qp, kp, vp, mla_wo[j])
            qs = mla_query(hs, mla_wdq[j], mla_qa_norm[j], mla_wuq[j], mla_q_norm[j], pos)
            ckv_s, kpe_s = mla_compress(hs, mla_wdkv[j], mla_kv_norm[j])
            k_lat, v_lat = mla_expand(ckv_s, kpe_s, mla_wukv[j], mla_k_norm[j], pos)
            k_ctx, v_ctx = mla_expand(cache_mla_ckv[:, j].astype(ckv_s.dtype), cache_mla_kpe[:, j].astype(kpe_s.dtype),
                                      mla_wukv[j], mla_k_norm[j], None)
            os_ = mla_out(qs, jnp.concatenate([k_lat, k_ctx], axis=1),
                          jnp.concatenate([v_lat, v_ctx], axis=1), mla_wo[j])
        xp = xp + p_g1 * op
        xs = xs + s_g1 * os_

        hp = modulate(xp, norm2[l], p_sh2, p_sc2)
        hs = modulate(xs, norm2[l], s_sh2, s_sc2)
        xp = xp + p_g2 * peer(hp, peer_wq[l], peer_keys[l], peer_u[l], peer_v[l])
        xs = xs + s_g2 * peer(hs, peer_wq[l], peer_keys[l], peer_u[l], peer_v[l])

    return (xp, xs, jnp.stack(new_da_k, axis=1), jnp.stack(new_da_v, axis=1),
            jnp.stack(new_ckv, axis=1), jnp.stack(new_kpe, axis=1))
```

```python
import functools
import math

import jax
import jax.numpy as jnp
from jax import lax
from jax.experimental import pallas as pl
from jax.experimental.pallas import tpu as pltpu

GRID_W = 64
DA_HEAD_DIM = 64
DA_V_DIM = 2 * DA_HEAD_DIM
MLA_NOPE = 64
MLA_ROPE = 32
MLA_QK_DIM = MLA_NOPE + MLA_ROPE
MLA_V_DIM = 64
PEER_N_KEYS = 128
PEER_HALF = 128
PEER_TOPK = 16
ROPE_BASE = 10000.0
EPS = 1e-6

LANE = 128
VMEM_LIMIT = 56 * 1024 * 1024
TOK_TILE = 256
Q_TILE = 256
PEER_A_TILE = 256
PEER_B_TILE = 512
PEER_E_TILE = 1024

BF16 = jnp.bfloat16
F32 = jnp.float32


def _params(sem):
    return pltpu.CompilerParams(dimension_semantics=sem, vmem_limit_bytes=VMEM_LIMIT)


def _dot(a, b):
    return jnp.dot(a, b, preferred_element_type=F32)


def _dot_nt(a, b):
    return lax.dot_general(a, b, (((1,), (1,)), ((), ())), preferred_element_type=F32)


def _rms(x):
    return x * lax.rsqrt(jnp.mean(x * x, axis=-1, keepdims=True) + EPS)


def _modulate(x, g, shift, scale):
    return _rms(x) * g * (1.0 + scale) + shift


def _rope(x, cos, sin, half):
    lane = lax.broadcasted_iota(jnp.int32, x.shape, 1)
    first = (lane & (2 * half - 1)) < half
    partner = jnp.where(first, pltpu.roll(x, LANE - half, 1), pltpu.roll(x, half, 1))
    return x * cos + partner * sin


class _Rows:
    def __init__(self, n_p, n_b, s_lat):
        self.n_p, self.n_b, self.s_lat = n_p, n_b, s_lat
        self.n = n_p + n_b * s_lat

    def mod_row(self, i, tm):
        np_t = self.n_p // tm
        return jnp.where(i < np_t, 0, 1 + (i - np_t) // (self.s_lat // tm))

    def pos_blk(self, i, tm):
        np_t = self.n_p // tm
        return jnp.where(i < np_t, self.s_lat // tm, (i - np_t) % (self.s_lat // tm))


def _mod_spec(rows, tm, which, d):
    return pl.BlockSpec((1, 1, d), lambda i: (rows.mod_row(i, tm) * 6 + which, 0, 0))


def _rope_tables(s_lat, tm, rot_lo, rot_w, period):
    hd = rot_w // 2
    half = hd // 2
    lane = jnp.arange(LANE)
    d = (lane % period) - rot_lo
    rot = (d >= 0) & (d < rot_w)
    d = jnp.clip(d, 0, rot_w - 1)
    use_row = d < hd
    dd = d % hd
    i = dd % half
    first = dd < half
    inv = 1.0 / (ROPE_BASE ** ((2 * i).astype(F32) / hd))
    t = jnp.arange(s_lat)
    row = (t // GRID_W).astype(F32)
    col = (t % GRID_W).astype(F32)
    pos = jnp.where(use_row[None, :], row[:, None], col[:, None])
    ang = pos * inv[None, :]
    cos = jnp.where(rot[None, :], jnp.cos(ang), 1.0)
    sin = jnp.where(rot[None, :], jnp.where(first[None, :], -jnp.sin(ang), jnp.sin(ang)), 0.0)
    cos = jnp.concatenate([cos, jnp.ones((tm, LANE), F32)], axis=0)
    sin = jnp.concatenate([sin, jnp.zeros((tm, LANE), F32)], axis=0)
    return cos.astype(F32), sin.astype(F32)


def _mod_kernel(c_ref, w_ref, b_ref, o_ref):
    c = c_ref[...]
    s = c * jax.nn.sigmoid(c)
    o_ref[...] = _dot(s.astype(BF16), w_ref[...].astype(BF16)) + b_ref[...]


def _modulation(cvecs, ada_w, ada_b):
    depth, d, d6 = ada_w.shape
    r = cvecs.shape[0]
    tn = d
    return pl.pallas_call(
        _mod_kernel,
        grid=(depth, d6 // tn),
        in_specs=[
            pl.BlockSpec((r, d), lambda l, j: (0, 0)),
            pl.BlockSpec((None, d, tn), lambda l, j: (l, 0, j)),
            pl.BlockSpec((None, 1, tn), lambda l, j: (l, 0, j)),
        ],
        out_specs=pl.BlockSpec((None, r, tn), lambda l, j: (l, 0, j)),
        out_shape=jax.ShapeDtypeStruct((depth, r, d6), F32),
        compiler_params=_params(("arbitrary", "arbitrary")),
        name="adaln_modulation",
    )(cvecs, ada_w, ada_b.reshape(depth, 1, d6))


def _da_qkv_kernel(x_ref, n_ref, sh_ref, sc_ref, wq_ref, wk_ref, wv_ref, qn_ref, kn_ref, cos_ref, sin_ref,
                   q_ref, kb_ref, vb_ref, kf_ref, vf_ref):
    h = _modulate(x_ref[...], n_ref[...], sh_ref[0], sc_ref[0]).astype(BF16)
    q = _dot(h, wq_ref[...])
    k = _dot(h, wk_ref[...])
    v = _dot(h, wv_ref[...])
    vf_ref[...] = v
    vb_ref[...] = v.astype(BF16)
    cos = cos_ref[...]
    sin = sin_ref[...]
    lane = lax.broadcasted_iota(jnp.int32, (x_ref.shape[0], LANE), 1)
    lo = lane < DA_HEAD_DIM

    def head_norm(t, g):
        sq = t * t
        s_lo = jnp.sum(jnp.where(lo, sq, 0.0), axis=-1, keepdims=True)
        s_hi = jnp.sum(jnp.where(lo, 0.0, sq), axis=-1, keepdims=True)
        ms = jnp.where(lo, s_lo, s_hi) * (1.0 / DA_HEAD_DIM)
        return t * lax.rsqrt(ms + EPS) * g

    scale = DA_HEAD_DIM ** -0.5
    for j in range(q.shape[1] // LANE):
        sl = slice(j * LANE, (j + 1) * LANE)
        qj = _rope(head_norm(q[:, sl], qn_ref[...]), cos, sin, DA_HEAD_DIM // 4)
        kj = _rope(head_norm(k[:, sl], kn_ref[...]), cos, sin, DA_HEAD_DIM // 4)
        q_ref[:, sl] = (qj * scale).astype(BF16)
        kf_ref[:, sl] = kj
        kb_ref[:, sl] = kj.astype(BF16)


def _da_qkv(x, rows, norm, modarr, wq, wk, wv, qn, kn, cos, sin):
    n, d = x.shape
    tm = TOK_TILE
    w = wq.shape[1]
    full = lambda shape: pl.BlockSpec(shape, lambda i: (0,) * len(shape))
    tok = lambda width: pl.BlockSpec((tm, width), lambda i: (i, 0))
    tab = pl.BlockSpec((tm, LANE), lambda i: (rows.pos_blk(i, tm), 0))
    return pl.pallas_call(
        _da_qkv_kernel,
        grid=(n // tm,),
        in_specs=[tok(d), full((1, d)), _mod_spec(rows, tm, 0, d), _mod_spec(rows, tm, 1, d),
                  full((d, w)), full((d, w)), full((d, w)), full((1, LANE)), full((1, LANE)), tab, tab],
        out_specs=[tok(w)] * 5,
        out_shape=[jax.ShapeDtypeStruct((n, w), BF16)] * 3 + [jax.ShapeDtypeStruct((n, w), F32)] * 2,
        compiler_params=_params(("arbitrary",)),
        name="da_qkv",
    )(x, norm, modarr, modarr, wq, wk, wv, qn, kn, cos, sin)


def _softmax_pv(q, kv_refs, ksl=slice(None)):
    scores = [_dot_nt(q, k_ref[:, ksl].astype(BF16)) for k_ref, _ in kv_refs]
    m = scores[0].max(axis=-1, keepdims=True)
    for s in scores[1:]:
        m = jnp.maximum(m, s.max(axis=-1, keepdims=True))
    num = None
    den = None
    for s, (_, v_ref) in zip(scores, kv_refs):
        p = jnp.exp(s - m)
        dsum = p.sum(axis=-1, keepdims=True)
        o = _dot(p.astype(BF16), v_ref[...].astype(BF16))
        num = o if num is None else num + o
        den = dsum if den is None else den + dsum
    return num / den


def _da_attn_kernel(lam_init, n_chunks, lam_ref, subln_ref, q_ref, *refs):
    kv = [(refs[2 * c], refs[2 * c + 1]) for c in range(n_chunks)]
    o_ref = refs[2 * n_chunks]
    lv = lam_ref[...]
    lam = (jnp.exp(jnp.sum(lv[0:1] * lv[1:2], axis=-1, keepdims=True))
           - jnp.exp(jnp.sum(lv[2:3] * lv[3:4], axis=-1, keepdims=True)) + lam_init)
    q = q_ref[...]
    lane = lax.broadcasted_iota(jnp.int32, q.shape, 1)
    zero = jnp.zeros_like(q)
    o1 = _softmax_pv(jnp.where(lane < DA_HEAD_DIM, q, zero), kv)
    o2 = _softmax_pv(jnp.where(lane < DA_HEAD_DIM, zero, q), kv)
    o = o1 - lam * o2
    o = _rms(o) * subln_ref[...] * (1.0 - lam_init)
    o_ref[...] = o.astype(o_ref.dtype)


def _mla_attn_kernel(n_chunks, q_ref, *refs):
    kv = [(refs[2 * c], refs[2 * c + 1]) for c in range(n_chunks)]
    o_ref = refs[2 * n_chunks]
    outs = []
    for half in range(2):
        sl = slice(half * LANE, (half + 1) * LANE)
        outs.append(_softmax_pv(q_ref[:, sl], kv, sl))
    lane = lax.broadcasted_iota(jnp.int32, outs[0].shape, 1)
    o_ref[...] = jnp.where(lane < MLA_V_DIM, outs[0], outs[1]).astype(o_ref.dtype)


def _attention(kernel, n_pairs, qw, q, seq, row_off, n_b, chunks, extra_in, extra_specs, name):
    tq = min(Q_TILE, seq)
    nq = seq // tq
    q_spec = pl.BlockSpec((tq, qw), lambda b, hp, qi: (row_off // tq + b * nq + qi, hp))
    in_specs = list(extra_specs) + [q_spec]
    args = list(extra_in) + [q]
    for k, v, sk, off in chunks:
        in_specs.append(pl.BlockSpec((sk, qw), functools.partial(lambda b, hp, qi, o, s: (o // s + b, hp), o=off, s=sk)))
        in_specs.append(pl.BlockSpec((sk, LANE), functools.partial(lambda b, hp, qi, o, s: (o // s + b, hp), o=off, s=sk)))
        args += [k, v]
    return pl.pallas_call(
        kernel,
        grid=(n_b, n_pairs, nq),
        in_specs=in_specs,
        out_specs=pl.BlockSpec((tq, LANE), lambda b, hp, qi: (b * nq + qi, hp)),
        out_shape=jax.ShapeDtypeStruct((n_b * seq, n_pairs * LANE), BF16),
        compiler_params=_params(("arbitrary", "arbitrary", "arbitrary")),
        name=name,
    )(*args)


def _out_proj_kernel(o_ref, w_ref, x_ref, g_ref, y_ref):
    y_ref[...] = x_ref[...] + g_ref[0] * _dot(o_ref[...], w_ref[...])


def _out_proj(o, w, x, rows, modarr, which):
    n, d = x.shape
    tm = TOK_TILE
    k = o.shape[1]
    return pl.pallas_call(
        _out_proj_kernel,
        grid=(n // tm,),
        in_specs=[pl.BlockSpec((tm, k), lambda i: (i, 0)), pl.BlockSpec((k, d), lambda i: (0, 0)),
                  pl.BlockSpec((tm, d), lambda i: (i, 0)), _mod_spec(rows, tm, which, d)],
        out_specs=pl.BlockSpec((tm, d), lambda i: (i, 0)),
        out_shape=jax.ShapeDtypeStruct((n, d), F32),
        compiler_params=_params(("arbitrary",)),
        name="out_proj",
    )(o, w, x, modarr)


def _head_norm128(t, g, width):
    ms = jnp.sum(t * t, axis=-1, keepdims=True) * (1.0 / width)
    return t * lax.rsqrt(ms + EPS) * g


def _mla_q_kernel(kv_lora, x_ref, n_ref, sh_ref, sc_ref, wdq_ref, qan_ref, wuq_ref, qn_ref, wdkv_ref, kvn_ref,
                  cos_ref, sin_ref, q_ref, ckv_ref, kpe_ref):
    h = _modulate(x_ref[...], n_ref[...], sh_ref[0], sc_ref[0]).astype(BF16)
    cq = (_rms(_dot(h, wdq_ref[...])) * qan_ref[...]).astype(BF16)
    q = _dot(cq, wuq_ref[...])
    cos = cos_ref[...]
    sin = sin_ref[...]
    scale = MLA_QK_DIM ** -0.5
    for j in range(q.shape[1] // LANE):
        sl = slice(j * LANE, (j + 1) * LANE)
        qj = _rope(_head_norm128(q[:, sl], qn_ref[...], MLA_QK_DIM), cos, sin, MLA_ROPE // 4)
        q_ref[:, sl] = (qj * scale).astype(BF16)
    ckvf = _dot(h, wdkv_ref[...])
    ckv_ref[...] = _rms(ckvf[:, :kv_lora]) * kvn_ref[...]
    kpe_ref[...] = ckvf[:, kv_lora:]


def _mla_q(x, rows, norm, modarr, wdq, qan, wuq, qn, wdkv, kvn, cos, sin):
    n, d = x.shape
    tm = TOK_TILE
    ql = wdq.shape[1]
    qw = wuq.shape[1]
    kv_lora = kvn.shape[1]
    full = lambda shape: pl.BlockSpec(shape, lambda i: (0,) * len(shape))
    tok = lambda width: pl.BlockSpec((tm, width), lambda i: (i, 0))
    tab = pl.BlockSpec((tm, LANE), lambda i: (rows.pos_blk(i, tm), 0))
    return pl.pallas_call(
        functools.partial(_mla_q_kernel, kv_lora),
        grid=(n // tm,),
        in_specs=[tok(d), full((1, d)), _mod_spec(rows, tm, 0, d), _mod_spec(rows, tm, 1, d),
                  full((d, ql)), full((1, ql)), full((ql, qw)), full((1, LANE)),
                  full((d, kv_lora + LANE)), full((1, kv_lora)), tab, tab],
        out_specs=[tok(qw), tok(kv_lora), tok(LANE)],
        out_shape=[jax.ShapeDtypeStruct((n, qw), BF16), jax.ShapeDtypeStruct((n, kv_lora), F32),
                   jax.ShapeDtypeStruct((n, LANE), F32)],
        compiler_params=_params(("arbitrary",)),
        name="mla_q_ckv",
    )(x, norm, modarr, modarr, wdq, qan, wuq, qn, wdkv, kvn, cos, sin)


def _mla_kv_kernel(ckv_ref, kpe_ref, wuk_ref, wuv_ref, kn_ref, cos_ref, sin_ref, k_ref, v_ref):
    c = ckv_ref[...].astype(BF16)
    kn = _dot(c, wuk_ref[...])
    v_ref[...] = _dot(c, wuv_ref[...]).astype(BF16)
    kpe = kpe_ref[...]
    cos = cos_ref[...]
    sin = sin_ref[...]
    for j in range(kn.shape[1] // LANE):
        sl = slice(j * LANE, (j + 1) * LANE)
        kj = _rope(_head_norm128(kn[:, sl] + kpe, kn_ref[...], MLA_QK_DIM), cos, sin, MLA_ROPE // 4)
        k_ref[:, sl] = kj.astype(BF16)


def _mla_kv(ckv, kpe, wuk, wuv, kn, cos, sin, pos_blk):
    n, kv_lora = ckv.shape
    tm = TOK_TILE
    kw = wuk.shape[1]
    vw = wuv.shape[1]
    full = lambda shape: pl.BlockSpec(shape, lambda i: (0,) * len(shape))
    tok = lambda width: pl.BlockSpec((tm, width), lambda i: (i, 0))
    tab = pl.BlockSpec((tm, LANE), lambda i: (pos_blk(i), 0))
    return pl.pallas_call(
        _mla_kv_kernel,
        grid=(n // tm,),
        in_specs=[tok(kv_lora), tok(LANE), full((kv_lora, kw)), full((kv_lora, vw)), full((1, LANE)), tab, tab],
        out_specs=[tok(kw), tok(vw)],
        out_shape=[jax.ShapeDtypeStruct((n, kw), BF16), jax.ShapeDtypeStruct((n, vw), BF16)],
        compiler_params=_params(("arbitrary",)),
        name="mla_kv_expand",
    )(ckv, kpe, wuk, wuv, kn, cos, sin)


def _first_max(cur, iota):
    mx = jnp.max(cur, axis=0, keepdims=True)
    idx = jnp.min(jnp.where(cur == mx, iota, cur.shape[0]), axis=0, keepdims=True)
    return mx, iota == idx


def _peer_a_kernel(n_heads, x_ref, n_ref, sh_ref, sc_ref, wqt_ref, keys_ref,
                   ht_ref, e1_ref, e0n_ref, tau_ref, qt_sc, e_sc, top_sc):
    t = x_ref.shape[0]
    k = PEER_TOPK
    h = _modulate(x_ref[...], n_ref[...], sh_ref[0], sc_ref[0])
    ht = h.T.astype(BF16)
    ht_ref[...] = ht
    qt_sc[...] = _dot(wqt_ref[...], ht)
    iota_n = lax.broadcasted_iota(jnp.int32, (PEER_N_KEYS, t), 0)

    def per_set(hc, carry):
        qb = qt_sc[pl.ds(pl.multiple_of(hc * PEER_HALF, PEER_HALF), PEER_HALF), :]
        s = _dot(keys_ref[hc], qb.astype(BF16))
        e = jnp.exp(s - jnp.max(s, axis=0, keepdims=True))
        e_sc[hc] = e

        def take(r, cur):
            mx, hit = _first_max(cur, iota_n)
            top_sc[hc, pl.ds(r, 1), :] = mx
            return jnp.where(hit, -1.0, cur)

        lax.fori_loop(0, k, take, e)
        return carry

    lax.fori_loop(0, 2 * n_heads, per_set, 0)
    iota_c = lax.broadcasted_iota(jnp.int32, (k * k, t), 0)

    def per_head(hh, carry):
        a = top_sc[2 * hh]
        b = top_sc[2 * hh + 1]
        cand = jnp.concatenate([a[r:r + 1] * b for r in range(k)], axis=0)

        def take(r, st):
            cur, z, _ = st
            mx, hit = _first_max(cur, iota_c)
            return jnp.where(hit, -1.0, cur), z + mx, jnp.where(hit, 1.0, 0.0)

        _, z, last = lax.fori_loop(0, k, take, (cand, jnp.zeros((1, t), F32), jnp.zeros((k * k, t), F32)))
        rz = 1.0 / z
        e0n_ref[hh] = e_sc[2 * hh] * rz
        e1_ref[hh] = e_sc[2 * hh + 1]
        an = a * rz
        candn = jnp.concatenate([an[r:r + 1] * b for r in range(k)], axis=0)
        tau_ref[pl.ds(hh, 1), :] = jnp.sum(last * candn, axis=0, keepdims=True)
        return carry

    lax.fori_loop(0, n_heads, per_head, 0)


def _peer_a(x, rows, norm, modarr, wqt, keys):
    n, d = x.shape
    t = PEER_A_TILE
    n_sets = keys.shape[0]
    n_heads = n_sets // 2
    full = lambda shape: pl.BlockSpec(shape, lambda i: (0,) * len(shape))
    return pl.pallas_call(
        functools.partial(_peer_a_kernel, n_heads),
        grid=(n // t,),
        in_specs=[pl.BlockSpec((t, d), lambda i: (i, 0)), full((1, d)),
                  _mod_spec(rows, t, 3, d), _mod_spec(rows, t, 4, d),
                  full(wqt.shape), full(keys.shape)],
        out_specs=[pl.BlockSpec((d, t), lambda i: (0, i)),
                   pl.BlockSpec((n_heads, PEER_N_KEYS, t), lambda i: (0, 0, i)),
                   pl.BlockSpec((n_heads, PEER_N_KEYS, t), lambda i: (0, 0, i)),
                   pl.BlockSpec((n_heads, t), lambda i: (0, i))],
        out_shape=[jax.ShapeDtypeStruct((d, n), BF16),
                   jax.ShapeDtypeStruct((n_heads, PEER_N_KEYS, n), F32),
                   jax.ShapeDtypeStruct((n_heads, PEER_N_KEYS, n), F32),
                   jax.ShapeDtypeStruct((n_heads, n), F32)],
        scratch_shapes=[pltpu.VMEM((n_sets * PEER_HALF, t), F32),
                        pltpu.VMEM((n_sets, PEER_N_KEYS, t), F32),
                        pltpu.VMEM((n_sets, PEER_TOPK, t), F32)],
        compiler_params=_params(("arbitrary",)),
        name="peer_select",
    )(x, norm, modarr, modarr, wqt, keys)


def _peer_b_kernel(n_heads, ht_ref, u_ref, vt_ref, e1_ref, e0n_ref, tau_ref, x_ref, g_ref, y_ref,
                   acc_sc, at_sc, w_sc):
    e = pl.program_id(1)
    n_i = u_ref.shape[0] // PEER_N_KEYS

    @pl.when(e == 0)
    def _():
        acc_sc[...] = jnp.zeros_like(acc_sc)

    at_sc[...] = _dot(u_ref[...], ht_ref[...])

    def per_i(ii, carry):
        rows = pl.ds(pl.multiple_of(ii * PEER_N_KEYS, PEER_N_KEYS), PEER_N_KEYS)
        gate = None
        for hh in range(n_heads):
            p = e1_ref[hh] * e0n_ref[hh, pl.ds(ii, 1), :]
            g = jnp.where(p >= tau_ref[pl.ds(hh, 1), :], p, 0.0)
            gate = g if gate is None else gate + g
        w_sc[rows, :] = (gate * jax.nn.gelu(at_sc[rows, :])).astype(BF16)
        return carry

    lax.fori_loop(0, n_i, per_i, 0)
    acc_sc[...] += _dot(vt_ref[...], w_sc[...])

    @pl.when(e == pl.num_programs(1) - 1)
    def _():
        y_ref[...] = x_ref[...] + g_ref[0] * acc_sc[...].T


def _peer_b(ht, u, vt, e1, e0n, tau, x, rows, modarr):
    n, d = x.shape
    t = PEER_B_TILE
    te = PEER_E_TILE
    n_exp = u.shape[0]
    n_heads = e1.shape[0]
    ti = te // PEER_N_KEYS
    return pl.pallas_call(
        functools.partial(_peer_b_kernel, n_heads),
        grid=(n // t, n_exp // te),
        in_specs=[pl.BlockSpec((d, t), lambda i, e: (0, i)),
                  pl.BlockSpec((te, d), lambda i, e: (e, 0)),
                  pl.BlockSpec((d, te), lambda i, e: (0, e)),
                  pl.BlockSpec((n_heads, PEER_N_KEYS, t), lambda i, e: (0, 0, i)),
                  pl.BlockSpec((n_heads, ti, t), lambda i, e: (0, e, i)),
                  pl.BlockSpec((n_heads, t), lambda i, e: (0, i)),
                  pl.BlockSpec((t, d), lambda i, e: (i, 0)),
                  pl.BlockSpec((1, 1, d), lambda i, e: (rows.mod_row(i, t) * 6 + 5, 0, 0))],
        out_specs=pl.BlockSpec((t, d), lambda i, e: (i, 0)),
        out_shape=jax.ShapeDtypeStruct((n, d), F32),
        scratch_shapes=[pltpu.VMEM((d, t), F32), pltpu.VMEM((te, t), F32), pltpu.VMEM((te, t), BF16)],
        compiler_params=_params(("arbitrary", "arbitrary")),
        name="peer_experts",
    )(ht, u, vt, e1, e0n, tau, x, modarr)


def _pad_heads(w, n_heads, width):
    k = w.shape[0]
    w = w.reshape(k, n_heads, width)
    return jnp.pad(w, ((0, 0), (0, 0), (0, LANE - width))).reshape(k, n_heads * LANE)


def _pad_lanes(g, lo=0):
    g = g.reshape(1, -1).astype(F32)
    return jnp.pad(g, ((0, 0), (lo, LANE - lo - g.shape[1])))


def kernel(x_prompt, x_sample, cache_da_k, cache_da_v, cache_mla_ckv, cache_mla_kpe, c, c_ctx, norm1, norm2, ada_w, ada_b, da_wq, da_wk, da_wv, da_wo, da_q_norm, da_k_norm, da_lq1, da_lk1, da_lq2, da_lk2, da_subln, mla_wdq, mla_qa_norm, mla_wuq, mla_wdkv, mla_kv_norm, mla_wukv, mla_q_norm, mla_k_norm, mla_wo, peer_wq, peer_keys, peer_u, peer_v):
    bp, sp, d = x_prompt.shape
    bs, ss, _ = x_sample.shape
    depth = norm1.shape[0]
    past = cache_da_k.shape[2]
    n_p = bp * sp
    rows = _Rows(n_p, bs, ss)
    tm = TOK_TILE
    assert sp % tm == 0 and ss % tm == 0 and n_p % ss == 0 and past == sp
    assert n_p % PEER_B_TILE == 0 and ss % PEER_B_TILE == 0

    da_heads = da_wv.shape[2] // DA_V_DIM
    mla_heads = mla_wo.shape[1] // MLA_V_DIM
    kv_lora = mla_kv_norm.shape[1]
    peer_heads = peer_keys.shape[1]

    x = jnp.concatenate([x_prompt.reshape(n_p, d), x_sample.reshape(bs * ss, d)], axis=0)

    n_cond = 1 + bs
    r_pad = -(-n_cond // 8) * 8
    cvecs = jnp.concatenate([c_ctx[None, :], c, jnp.zeros((r_pad - n_cond, d), F32)], axis=0)
    mods = _modulation(cvecs, ada_w, ada_b).reshape(depth, r_pad * 6, 1, d)

    da_cos, da_sin = _rope_tables(ss, tm, 0, DA_HEAD_DIM, DA_HEAD_DIM)
    mla_cos, mla_sin = _rope_tables(ss, tm, MLA_NOPE, MLA_ROPE, LANE)
    ident = ss // tm

    new_da_k, new_da_v, new_ckv, new_kpe = [], [], [], []
    for l in range(depth):
        modarr = mods[l]
        j = l // 2
        if l % 2 == 0:
            lam_init = 0.8 - 0.6 * math.exp(-0.3 * l)
            q, kb, vb, kf, vf = _da_qkv(
                x, rows, norm1[l][None, :], modarr,
                da_wq[j].astype(BF16), da_wk[j].astype(BF16), da_wv[j].astype(BF16),
                jnp.tile(da_q_norm[j][None, :], (1, 2)), jnp.tile(da_k_norm[j][None, :], (1, 2)), da_cos, da_sin)
            new_da_k.append(kf[:n_p].reshape(bp, sp, 2 * da_heads, DA_HEAD_DIM))
            new_da_v.append(vf[:n_p].reshape(bp, sp, da_heads, DA_V_DIM))
            lam_vecs = jnp.stack([da_lq1[j], da_lk1[j], da_lq2[j], da_lk2[j]]).astype(F32)
            extra = [lam_vecs, da_subln[j][None, :]]
            extra_specs = [pl.BlockSpec(lam_vecs.shape, lambda b, hp, qi: (0, 0)),
                           pl.BlockSpec((1, DA_V_DIM), lambda b, hp, qi: (0, 0))]
            ck = cache_da_k[:, j].reshape(bs * past, 2 * da_heads * DA_HEAD_DIM)
            cv = cache_da_v[:, j].reshape(bs * past, da_heads * DA_V_DIM)
            o_p = _attention(functools.partial(_da_attn_kernel, lam_init, 1), da_heads, LANE, q, sp, 0, bp,
                             [(kb, vb, sp, 0)], extra, extra_specs, "da_attn_prompt")
            o_s = _attention(functools.partial(_da_attn_kernel, lam_init, 2), da_heads, LANE, q, ss, n_p, bs,
                             [(kb, vb, ss, n_p), (ck, cv, past, 0)], extra, extra_specs, "da_attn_latent")
            wo = da_wo[j].astype(BF16)
        else:
            wuq = _pad_heads(mla_wuq[j], mla_heads, MLA_QK_DIM).astype(BF16)
            wdkv = jnp.concatenate([
                mla_wdkv[j][:, :kv_lora],
                jnp.pad(mla_wdkv[j][:, kv_lora:], ((0, 0), (MLA_NOPE, LANE - MLA_QK_DIM)))], axis=1).astype(BF16)
            wukv = mla_wukv[j].reshape(kv_lora, mla_heads, MLA_NOPE + MLA_V_DIM)
            wuk = jnp.pad(wukv[:, :, :MLA_NOPE], ((0, 0), (0, 0), (0, LANE - MLA_NOPE)))
            wuk = wuk.reshape(kv_lora, mla_heads * LANE).astype(BF16)
            wuv = wukv[:, :, MLA_NOPE:].reshape(kv_lora, mla_heads * MLA_V_DIM).astype(BF16)
            q, ckv, kpe = _mla_q(
                x, rows, norm1[l][None, :], modarr, mla_wdq[j].astype(BF16), mla_qa_norm[j][None, :], wuq,
                _pad_lanes(mla_q_norm[j]), wdkv, mla_kv_norm[j][None, :], mla_cos, mla_sin)
            new_ckv.append(ckv[:n_p].reshape(bp, sp, kv_lora))
            new_kpe.append(kpe[:n_p, MLA_NOPE:MLA_QK_DIM].reshape(bp, sp, MLA_ROPE))
            kn = _pad_lanes(mla_k_norm[j])
            k, v = _mla_kv(ckv, kpe, wuk, wuv, kn, mla_cos, mla_sin, lambda i: rows.pos_blk(i, tm))
            c_ckv = cache_mla_ckv[:, j].reshape(bs * past, kv_lora)
            c_kpe = jnp.pad(cache_mla_kpe[:, j].reshape(bs * past, MLA_ROPE), ((0, 0), (MLA_NOPE, LANE - MLA_QK_DIM)))
            ck, cv = _mla_kv(c_ckv, c_kpe, wuk, wuv, kn, mla_cos, mla_sin, lambda i: ident)
            o_p = _attention(functools.partial(_mla_attn_kernel, 1), mla_heads // 2, 2 * LANE, q, sp, 0, bp,
                             [(k, v, sp, 0)], [], [], "mla_attn_prompt")
            o_s = _attention(functools.partial(_mla_attn_kernel, 2), mla_heads // 2, 2 * LANE, q, ss, n_p, bs,
                             [(k, v, ss, n_p), (ck, cv, past, 0)], [], [], "mla_attn_latent")
            wo = mla_wo[j].astype(BF16)
        x = _out_proj(jnp.concatenate([o_p, o_s], axis=0), wo, x, rows, modarr, 2)

        wqt = peer_wq[l].T.astype(BF16)
        keys = peer_keys[l].reshape(2 * peer_heads, PEER_N_KEYS, PEER_HALF).astype(BF16)
        ht, e1, e0n, tau = _peer_a(x, rows, norm2[l][None, :], modarr, wqt, keys)
        x = _peer_b(ht, peer_u[l].astype(BF16), peer_v[l].T.astype(BF16), e1, e0n, tau, x, rows, modarr)

    return (x[:n_p].reshape(bp, sp, d), x[n_p:].reshape(bs, ss, d),
            jnp.stack(new_da_k, axis=1), jnp.stack(new_da_v, axis=1),
            jnp.stack(new_ckv, axis=1), jnp.stack(new_kpe, axis=1))
```

```python
import functools
import math

import jax
import jax.numpy as jnp
from jax import lax
from jax.experimental import pallas as pl
from jax.experimental.pallas import tpu as pltpu

GRID_W = 64
DA_HEAD_DIM = 64
DA_V_DIM = 2 * DA_HEAD_DIM
MLA_NOPE = 64
MLA_ROPE = 32
MLA_QK_DIM = MLA_NOPE + MLA_ROPE
MLA_V_DIM = 64
PEER_N_KEYS = 128
PEER_HALF = 128
PEER_TOPK = 16
ROPE_BASE = 10000.0
EPS = 1e-6

LANE = 128
VMEM_LIMIT = 56 * 1024 * 1024
TOK_TILE = 256
Q_TILE = 256
PEER_A_TILE = 256
PEER_B_TILE = 512
PEER_E_TILE = 1024
PEER_B_CHUNK = 256

BF16 = jnp.bfloat16
F32 = jnp.float32


def _params(sem, flags=None):
    return pltpu.CompilerParams(dimension_semantics=sem, vmem_limit_bytes=VMEM_LIMIT, flags=flags)


def _dot(a, b):
    return jnp.dot(a, b, preferred_element_type=F32)


def _dot_nt(a, b):
    return lax.dot_general(a, b, (((1,), (1,)), ((), ())), preferred_element_type=F32)


def _rms(x):
    return x * lax.rsqrt(jnp.mean(x * x, axis=-1, keepdims=True) + EPS)


def _modulate(x, g, shift, scale):
    return _rms(x) * g * (1.0 + scale) + shift


def _rope(x, cos, sin, half):
    lane = lax.broadcasted_iota(jnp.int32, x.shape, 1)
    first = (lane & (2 * half - 1)) < half
    partner = jnp.where(first, pltpu.roll(x, LANE - half, 1), pltpu.roll(x, half, 1))
    return x * cos + partner * sin


class _Rows:
    def __init__(self, n_p, n_b, s_lat):
        self.n_p, self.n_b, self.s_lat = n_p, n_b, s_lat
        self.n = n_p + n_b * s_lat

    def mod_row(self, i, tm):
        np_t = self.n_p // tm
        return jnp.where(i < np_t, 0, 1 + (i - np_t) // (self.s_lat // tm))

    def pos_blk(self, i, tm):
        np_t = self.n_p // tm
        return jnp.where(i < np_t, self.s_lat // tm, (i - np_t) % (self.s_lat // tm))


def _mod_spec(rows, tm, which, d):
    return pl.BlockSpec((1, 1, d), lambda i: (rows.mod_row(i, tm) * 6 + which, 0, 0))


def _rope_tables(s_lat, tm, rot_lo, rot_w, period):
    hd = rot_w // 2
    half = hd // 2
    lane = jnp.arange(LANE)
    d = (lane % period) - rot_lo
    rot = (d >= 0) & (d < rot_w)
    d = jnp.clip(d, 0, rot_w - 1)
    use_row = d < hd
    dd = d % hd
    i = dd % half
    first = dd < half
    inv = 1.0 / (ROPE_BASE ** ((2 * i).astype(F32) / hd))
    t = jnp.arange(s_lat)
    row = (t // GRID_W).astype(F32)
    col = (t % GRID_W).astype(F32)
    pos = jnp.where(use_row[None, :], row[:, None], col[:, None])
    ang = pos * inv[None, :]
    cos = jnp.where(rot[None, :], jnp.cos(ang), 1.0)
    sin = jnp.where(rot[None, :], jnp.where(first[None, :], -jnp.sin(ang), jnp.sin(ang)), 0.0)
    cos = jnp.concatenate([cos, jnp.ones((tm, LANE), F32)], axis=0)
    sin = jnp.concatenate([sin, jnp.zeros((tm, LANE), F32)], axis=0)
    return cos.astype(F32), sin.astype(F32)


def _mod_kernel(c_ref, w_ref, b_ref, o_ref):
    c = c_ref[...]
    s = c * jax.nn.sigmoid(c)
    o_ref[...] = _dot(s.astype(BF16), w_ref[...].astype(BF16)) + b_ref[...]


def _modulation(cvecs, ada_w, ada_b):
    depth, d, d6 = ada_w.shape
    r = cvecs.shape[0]
    tn = d
    return pl.pallas_call(
        _mod_kernel,
        grid=(depth, d6 // tn),
        in_specs=[
            pl.BlockSpec((r, d), lambda l, j: (0, 0)),
            pl.BlockSpec((None, d, tn), lambda l, j: (l, 0, j)),
            pl.BlockSpec((None, 1, tn), lambda l, j: (l, 0, j)),
        ],
        out_specs=pl.BlockSpec((None, r, tn), lambda l, j: (l, 0, j)),
        out_shape=jax.ShapeDtypeStruct((depth, r, d6), F32),
        compiler_params=_params(("arbitrary", "arbitrary")),
        name="adaln_modulation",
    )(cvecs, ada_w, ada_b.reshape(depth, 1, d6))


def _da_qkv_kernel(x_ref, n_ref, sh_ref, sc_ref, wq_ref, wk_ref, wv_ref, qn_ref, kn_ref, cos_ref, sin_ref,
                   q_ref, kb_ref, vb_ref, kf_ref, vf_ref):
    h = _modulate(x_ref[...], n_ref[...], sh_ref[0], sc_ref[0]).astype(BF16)
    q = _dot(h, wq_ref[...])
    k = _dot(h, wk_ref[...])
    v = _dot(h, wv_ref[...])
    vf_ref[...] = v
    vb_ref[...] = v.astype(BF16)
    cos = cos_ref[...]
    sin = sin_ref[...]
    lane = lax.broadcasted_iota(jnp.int32, (x_ref.shape[0], LANE), 1)
    lo = lane < DA_HEAD_DIM

    def head_norm(t, g):
        sq = t * t
        s_lo = jnp.sum(jnp.where(lo, sq, 0.0), axis=-1, keepdims=True)
        s_hi = jnp.sum(jnp.where(lo, 0.0, sq), axis=-1, keepdims=True)
        ms = jnp.where(lo, s_lo, s_hi) * (1.0 / DA_HEAD_DIM)
        return t * lax.rsqrt(ms + EPS) * g

    scale = DA_HEAD_DIM ** -0.5
    for j in range(q.shape[1] // LANE):
        sl = slice(j * LANE, (j + 1) * LANE)
        qj = _rope(head_norm(q[:, sl], qn_ref[...]), cos, sin, DA_HEAD_DIM // 4)
        kj = _rope(head_norm(k[:, sl], kn_ref[...]), cos, sin, DA_HEAD_DIM // 4)
        q_ref[:, sl] = (qj * scale).astype(BF16)
        kf_ref[:, sl] = kj
        kb_ref[:, sl] = kj.astype(BF16)


def _da_qkv(x, rows, norm, modarr, wq, wk, wv, qn, kn, cos, sin):
    n, d = x.shape
    tm = TOK_TILE
    w = wq.shape[1]
    full = lambda shape: pl.BlockSpec(shape, lambda i: (0,) * len(shape))
    tok = lambda width: pl.BlockSpec((tm, width), lambda i: (i, 0))
    tab = pl.BlockSpec((tm, LANE), lambda i: (rows.pos_blk(i, tm), 0))
    return pl.pallas_call(
        _da_qkv_kernel,
        grid=(n // tm,),
        in_specs=[tok(d), full((1, d)), _mod_spec(rows, tm, 0, d), _mod_spec(rows, tm, 1, d),
                  full((d, w)), full((d, w)), full((d, w)), full((1, LANE)), full((1, LANE)), tab, tab],
        out_specs=[tok(w)] * 5,
        out_shape=[jax.ShapeDtypeStruct((n, w), BF16)] * 3 + [jax.ShapeDtypeStruct((n, w), F32)] * 2,
        compiler_params=_params(("arbitrary",)),
        name="da_qkv",
    )(x, norm, modarr, modarr, wq, wk, wv, qn, kn, cos, sin)


def _softmax_pv(q, kv_refs, ksl=slice(None)):
    scores = [_dot_nt(q, k_ref[:, ksl].astype(BF16)) for k_ref, _ in kv_refs]
    m = scores[0].max(axis=-1, keepdims=True)
    for s in scores[1:]:
        m = jnp.maximum(m, s.max(axis=-1, keepdims=True))
    num = None
    den = None
    for s, (_, v_ref) in zip(scores, kv_refs):
        p = jnp.exp(s - m)
        dsum = p.sum(axis=-1, keepdims=True)
        o = _dot(p.astype(BF16), v_ref[...].astype(BF16))
        num = o if num is None else num + o
        den = dsum if den is None else den + dsum
    return num / den


def _da_attn_kernel(lam_init, n_chunks, lam_ref, subln_ref, q_ref, *refs):
    kv = [(refs[2 * c], refs[2 * c + 1]) for c in range(n_chunks)]
    o_ref = refs[2 * n_chunks]
    lv = lam_ref[...]
    lam = (jnp.exp(jnp.sum(lv[0:1] * lv[1:2], axis=-1, keepdims=True))
           - jnp.exp(jnp.sum(lv[2:3] * lv[3:4], axis=-1, keepdims=True)) + lam_init)
    q = q_ref[...]
    lane = lax.broadcasted_iota(jnp.int32, q.shape, 1)
    zero = jnp.zeros_like(q)
    o1 = _softmax_pv(jnp.where(lane < DA_HEAD_DIM, q, zero), kv)
    o2 = _softmax_pv(jnp.where(lane < DA_HEAD_DIM, zero, q), kv)
    o = o1 - lam * o2
    o = _rms(o) * subln_ref[...] * (1.0 - lam_init)
    o_ref[...] = o.astype(o_ref.dtype)


def _mla_attn_kernel(n_chunks, q_ref, *refs):
    kv = [(refs[2 * c], refs[2 * c + 1]) for c in range(n_chunks)]
    o_ref = refs[2 * n_chunks]
    outs = []
    for half in range(2):
        sl = slice(half * LANE, (half + 1) * LANE)
        outs.append(_softmax_pv(q_ref[:, sl], kv, sl))
    lane = lax.broadcasted_iota(jnp.int32, outs[0].shape, 1)
    o_ref[...] = jnp.where(lane < MLA_V_DIM, outs[0], outs[1]).astype(o_ref.dtype)


def _attention(kernel, n_pairs, qw, q, seq, row_off, n_b, chunks, extra_in, extra_specs, name):
    tq = min(Q_TILE, seq)
    nq = seq // tq
    q_spec = pl.BlockSpec((tq, qw), lambda b, hp, qi: (row_off // tq + b * nq + qi, hp))
    in_specs = list(extra_specs) + [q_spec]
    args = list(extra_in) + [q]
    for k, v, sk, off in chunks:
        in_specs.append(pl.BlockSpec((sk, qw), functools.partial(lambda b, hp, qi, o, s: (o // s + b, hp), o=off, s=sk)))
        in_specs.append(pl.BlockSpec((sk, LANE), functools.partial(lambda b, hp, qi, o, s: (o // s + b, hp), o=off, s=sk)))
        args += [k, v]
    return pl.pallas_call(
        kernel,
        grid=(n_b, n_pairs, nq),
        in_specs=in_specs,
        out_specs=pl.BlockSpec((tq, LANE), lambda b, hp, qi: (b * nq + qi, hp)),
        out_shape=jax.ShapeDtypeStruct((n_b * seq, n_pairs * LANE), BF16),
        compiler_params=_params(("arbitrary", "arbitrary", "arbitrary")),
        name=name,
    )(*args)


def _out_proj_kernel(o_ref, w_ref, x_ref, g_ref, y_ref):
    y_ref[...] = x_ref[...] + g_ref[0] * _dot(o_ref[...], w_ref[...])


def _out_proj(o, w, x, rows, modarr, which):
    n, d = x.shape
    tm = TOK_TILE
    k = o.shape[1]
    return pl.pallas_call(
        _out_proj_kernel,
        grid=(n // tm,),
        in_specs=[pl.BlockSpec((tm, k), lambda i: (i, 0)), pl.BlockSpec((k, d), lambda i: (0, 0)),
                  pl.BlockSpec((tm, d), lambda i: (i, 0)), _mod_spec(rows, tm, which, d)],
        out_specs=pl.BlockSpec((tm, d), lambda i: (i, 0)),
        out_shape=jax.ShapeDtypeStruct((n, d), F32),
        compiler_params=_params(("arbitrary",)),
        name="out_proj",
    )(o, w, x, modarr)


def _head_norm128(t, g, width):
    ms = jnp.sum(t * t, axis=-1, keepdims=True) * (1.0 / width)
    return t * lax.rsqrt(ms + EPS) * g


def _mla_q_kernel(kv_lora, x_ref, n_ref, sh_ref, sc_ref, wdq_ref, qan_ref, wuq_ref, qn_ref, wdkv_ref, kvn_ref,
                  cos_ref, sin_ref, q_ref, ckv_ref, kpe_ref):
    h = _modulate(x_ref[...], n_ref[...], sh_ref[0], sc_ref[0]).astype(BF16)
    cq = (_rms(_dot(h, wdq_ref[...])) * qan_ref[...]).astype(BF16)
    q = _dot(cq, wuq_ref[...])
    cos = cos_ref[...]
    sin = sin_ref[...]
    scale = MLA_QK_DIM ** -0.5
    for j in range(q.shape[1] // LANE):
        sl = slice(j * LANE, (j + 1) * LANE)
        qj = _rope(_head_norm128(q[:, sl], qn_ref[...], MLA_QK_DIM), cos, sin, MLA_ROPE // 4)
        q_ref[:, sl] = (qj * scale).astype(BF16)
    ckvf = _dot(h, wdkv_ref[...])
    ckv_ref[...] = _rms(ckvf[:, :kv_lora]) * kvn_ref[...]
    kpe_ref[...] = ckvf[:, kv_lora:]


def _mla_q(x, rows, norm, modarr, wdq, qan, wuq, qn, wdkv, kvn, cos, sin):
    n, d = x.shape
    tm = TOK_TILE
    ql = wdq.shape[1]
    qw = wuq.shape[1]
    kv_lora = kvn.shape[1]
    full = lambda shape: pl.BlockSpec(shape, lambda i: (0,) * len(shape))
    tok = lambda width: pl.BlockSpec((tm, width), lambda i: (i, 0))
    tab = pl.BlockSpec((tm, LANE), lambda i: (rows.pos_blk(i, tm), 0))
    return pl.pallas_call(
        functools.partial(_mla_q_kernel, kv_lora),
        grid=(n // tm,),
        in_specs=[tok(d), full((1, d)), _mod_spec(rows, tm, 0, d), _mod_spec(rows, tm, 1, d),
                  full((d, ql)), full((1, ql)), full((ql, qw)), full((1, LANE)),
                  full((d, kv_lora + LANE)), full((1, kv_lora)), tab, tab],
        out_specs=[tok(qw), tok(kv_lora), tok(LANE)],
        out_shape=[jax.ShapeDtypeStruct((n, qw), BF16), jax.ShapeDtypeStruct((n, kv_lora), F32),
                   jax.ShapeDtypeStruct((n, LANE), F32)],
        compiler_params=_params(("arbitrary",)),
        name="mla_q_ckv",
    )(x, norm, modarr, modarr, wdq, qan, wuq, qn, wdkv, kvn, cos, sin)


def _mla_kv_kernel(ckv_ref, kpe_ref, wuk_ref, wuv_ref, kn_ref, cos_ref, sin_ref, k_ref, v_ref):
    c = ckv_ref[...].astype(BF16)
    kn = _dot(c, wuk_ref[...])
    v_ref[...] = _dot(c, wuv_ref[...]).astype(BF16)
    kpe = kpe_ref[...]
    cos = cos_ref[...]
    sin = sin_ref[...]
    for j in range(kn.shape[1] // LANE):
        sl = slice(j * LANE, (j + 1) * LANE)
        kj = _rope(_head_norm128(kn[:, sl] + kpe, kn_ref[...], MLA_QK_DIM), cos, sin, MLA_ROPE // 4)
        k_ref[:, sl] = kj.astype(BF16)


def _mla_kv(ckv, kpe, wuk, wuv, kn, cos, sin, pos_blk):
    n, kv_lora = ckv.shape
    tm = TOK_TILE
    kw = wuk.shape[1]
    vw = wuv.shape[1]
    full = lambda shape: pl.BlockSpec(shape, lambda i: (0,) * len(shape))
    tok = lambda width: pl.BlockSpec((tm, width), lambda i: (i, 0))
    tab = pl.BlockSpec((tm, LANE), lambda i: (pos_blk(i), 0))
    return pl.pallas_call(
        _mla_kv_kernel,
        grid=(n // tm,),
        in_specs=[tok(kv_lora), tok(LANE), full((kv_lora, kw)), full((kv_lora, vw)), full((1, LANE)), tab, tab],
        out_specs=[tok(kw), tok(vw)],
        out_shape=[jax.ShapeDtypeStruct((n, kw), BF16), jax.ShapeDtypeStruct((n, vw), BF16)],
        compiler_params=_params(("arbitrary",)),
        name="mla_kv_expand",
    )(ckv, kpe, wuk, wuv, kn, cos, sin)


def _sort_pairs(n):
    pairs = []
    p = 1
    while p < n:
        k = p
        while k >= 1:
            for j in range(k % p, n - k, 2 * k):
                for i in range(min(k, n - j - k)):
                    if (i + j) // (2 * p) == (i + j + k) // (2 * p):
                        pairs.append((i + j, i + j + k))
            k //= 2
        p *= 2
    return pairs


def _cmpx(x, i, j):
    a, b = x[i], x[j]
    if b is None:
        return
    if a is None:
        x[i], x[j] = b, None
        return
    x[i], x[j] = jnp.maximum(a, b), jnp.minimum(a, b)


def _top_sorted(x):
    k = PEER_TOPK
    x = list(x)
    for i, j in _sort_pairs(k):
        _cmpx(x, i, j)
    shift = 4
    while shift >= 1:
        y = []
        for i in range(k):
            other = x[k - 1 - i]
            other = None if other is None else pltpu.roll(other, 8 - shift, 0)
            if x[i] is None:
                y.append(other)
            elif other is None:
                y.append(x[i])
            else:
                y.append(jnp.maximum(x[i], other))
        d = k // 2
        while d >= 1:
            for i in range(k):
                if (i & d) == 0:
                    _cmpx(y, i, i + d)
            d //= 2
        x = y
        shift //= 2
    return x


def _pair_candidates(a_ref, b_ref, scale):
    k = PEER_TOPK
    b_lo = b_ref[0:8, :]
    sub = lax.broadcasted_iota(jnp.int32, b_lo.shape, 0)
    row = lambda r: a_ref[r:r + 1, :] * scale
    out = [row(0) * b_lo, row(0) * b_ref[8:16, :]]
    for r in range(1, 8):
        n_r = k // (r + 1)
        c = row(r) * b_lo
        out.append(c if n_r >= 8 else jnp.where(sub < n_r, c, -1.0))
    out.append((a_ref[8:16, :] * scale) * b_ref[0:1, :])
    return out + [None] * (k - len(out))


def _peer_a_kernel(n_heads, x_ref, n_ref, sh_ref, sc_ref, wqt_ref, keys_ref,
                   ht_ref, e1_ref, e0n_ref, tau_ref, qt_sc, e_sc, top_sc):
    k = PEER_TOPK
    h = _modulate(x_ref[...], n_ref[...], sh_ref[0], sc_ref[0])
    ht = h.T.astype(BF16)
    ht_ref[...] = ht
    qt_sc[...] = _dot(wqt_ref[...], ht)

    def per_set(hc, carry):
        qb = qt_sc[pl.ds(pl.multiple_of(hc * PEER_HALF, PEER_HALF), PEER_HALF), :]
        s = _dot(keys_ref[hc], qb.astype(BF16))
        e = jnp.exp(s - jnp.max(s, axis=0, keepdims=True))
        e_sc[hc] = e
        top = _top_sorted([e[g * 8:(g + 1) * 8, :] for g in range(PEER_N_KEYS // 8)])
        for r in range(k):
            top_sc[hc, r:r + 1, :] = top[r][0:1, :]
        return carry

    lax.fori_loop(0, 2 * n_heads, per_set, 0)

    def per_head(hh, carry):
        a_ref = top_sc.at[2 * hh]
        b_ref = top_sc.at[2 * hh + 1]
        top = _top_sorted(_pair_candidates(a_ref, b_ref, 1.0))
        z = top[0][0:1, :]
        for r in range(1, k):
            z = z + top[r][0:1, :]
        rz = 1.0 / z
        e0n_ref[hh] = e_sc[2 * hh] * rz
        e1_ref[hh] = e_sc[2 * hh + 1]
        topn = _top_sorted(_pair_candidates(a_ref, b_ref, rz))
        tau_ref[pl.ds(hh, 1), :] = topn[k - 1][0:1, :]
        return carry

    lax.fori_loop(0, n_heads, per_head, 0)


def _peer_a(x, rows, norm, modarr, wqt, keys):
    n, d = x.shape
    t = PEER_A_TILE
    n_sets = keys.shape[0]
    n_heads = n_sets // 2
    full = lambda shape: pl.BlockSpec(shape, lambda i: (0,) * len(shape))
    return pl.pallas_call(
        functools.partial(_peer_a_kernel, n_heads),
        grid=(n // t,),
        in_specs=[pl.BlockSpec((t, d), lambda i: (i, 0)), full((1, d)),
                  _mod_spec(rows, t, 3, d), _mod_spec(rows, t, 4, d),
                  full(wqt.shape), full(keys.shape)],
        out_specs=[pl.BlockSpec((d, t), lambda i: (0, i)),
                   pl.BlockSpec((n_heads, PEER_N_KEYS, t), lambda i: (0, 0, i)),
                   pl.BlockSpec((n_heads, PEER_N_KEYS, t), lambda i: (0, 0, i)),
                   pl.BlockSpec((n_heads, t), lambda i: (0, i))],
        out_shape=[jax.ShapeDtypeStruct((d, n), BF16),
                   jax.ShapeDtypeStruct((n_heads, PEER_N_KEYS, n), F32),
                   jax.ShapeDtypeStruct((n_heads, PEER_N_KEYS, n), F32),
                   jax.ShapeDtypeStruct((n_heads, n), F32)],
        scratch_shapes=[pltpu.VMEM((n_sets * PEER_HALF, t), F32),
                        pltpu.VMEM((n_sets, PEER_N_KEYS, t), F32),
                        pltpu.VMEM((n_sets, PEER_TOPK, t), F32)],
        compiler_params=_params(("arbitrary",)),
        name="peer_select",
    )(x, norm, modarr, modarr, wqt, keys)


def _peer_b_kernel(n_heads, ht_ref, u_ref, vt_ref, e1_ref, e0n_ref, tau_ref, x_ref, g_ref, y_ref,
                   acc_sc, at_sc, w_sc):
    e = pl.program_id(1)
    n_i = u_ref.shape[0] // PEER_N_KEYS

    @pl.when(e == 0)
    def _():
        acc_sc[...] = jnp.zeros_like(acc_sc)

    ht = ht_ref[...]
    t = ht.shape[1]
    ch = PEER_B_CHUNK
    n_chunks = n_i * PEER_N_KEYS // ch

    def scores(c):
        at_sc[c * ch:(c + 1) * ch, :] = _dot(u_ref[c * ch:(c + 1) * ch, :], ht)

    scores(0)
    for c in range(n_chunks):
        if c + 1 < n_chunks:
            scores(c + 1)
        for r0 in range(c * ch, (c + 1) * ch, 64):
            ii, j0 = divmod(r0, PEER_N_KEYS)
            for l0 in range(0, t, LANE):
                ls = slice(l0, l0 + LANE)
                gate = None
                for hh in range(n_heads):
                    p = e1_ref[hh, j0:j0 + 64, ls] * e0n_ref[hh, ii:ii + 1, ls]
                    g = jnp.where(p >= tau_ref[hh:hh + 1, ls], p, 0.0)
                    gate = g if gate is None else gate + g
                w_sc[r0:r0 + 64, ls] = (gate * jax.nn.gelu(at_sc[r0:r0 + 64, ls])).astype(BF16)
        acc_sc[...] += _dot(vt_ref[:, c * ch:(c + 1) * ch], w_sc[c * ch:(c + 1) * ch, :])

    @pl.when(e == pl.num_programs(1) - 1)
    def _():
        y_ref[...] = x_ref[...] + g_ref[0] * acc_sc[...].T


def _peer_b(ht, u, vt, e1, e0n, tau, x, rows, modarr):
    n, d = x.shape
    t = PEER_B_TILE
    te = PEER_E_TILE
    n_exp = u.shape[0]
    n_heads = e1.shape[0]
    ti = te // PEER_N_KEYS
    return pl.pallas_call(
        functools.partial(_peer_b_kernel, n_heads),
        grid=(n // t, n_exp // te),
        in_specs=[pl.BlockSpec((d, t), lambda i, e: (0, i)),
                  pl.BlockSpec((te, d), lambda i, e: (e, 0)),
                  pl.BlockSpec((d, te), lambda i, e: (0, e)),
                  pl.BlockSpec((n_heads, PEER_N_KEYS, t), lambda i, e: (0, 0, i)),
                  pl.BlockSpec((n_heads, ti, t), lambda i, e: (0, e, i)),
                  pl.BlockSpec((n_heads, t), lambda i, e: (0, i)),
                  pl.BlockSpec((t, d), lambda i, e: (i, 0)),
                  pl.BlockSpec((1, 1, d), lambda i, e: (rows.mod_row(i, t) * 6 + 5, 0, 0))],
        out_specs=pl.BlockSpec((t, d), lambda i, e: (i, 0)),
        out_shape=jax.ShapeDtypeStruct((n, d), F32),
        scratch_shapes=[pltpu.VMEM((d, t), F32), pltpu.VMEM((te, t), F32), pltpu.VMEM((te, t), BF16)],
        compiler_params=_params(("arbitrary", "arbitrary")),
        name="peer_experts",
    )(ht, u, vt, e1, e0n, tau, x, modarr)


def _pad_heads(w, n_heads, width):
    k = w.shape[0]
    w = w.reshape(k, n_heads, width)
    return jnp.pad(w, ((0, 0), (0, 0), (0, LANE - width))).reshape(k, n_heads * LANE)


def _pad_lanes(g, lo=0):
    g = g.reshape(1, -1).astype(F32)
    return jnp.pad(g, ((0, 0), (lo, LANE - lo - g.shape[1])))


def kernel(x_prompt, x_sample, cache_da_k, cache_da_v, cache_mla_ckv, cache_mla_kpe, c, c_ctx, norm1, norm2, ada_w, ada_b, da_wq, da_wk, da_wv, da_wo, da_q_norm, da_k_norm, da_lq1, da_lk1, da_lq2, da_lk2, da_subln, mla_wdq, mla_qa_norm, mla_wuq, mla_wdkv, mla_kv_norm, mla_wukv, mla_q_norm, mla_k_norm, mla_wo, peer_wq, peer_keys, peer_u, peer_v):
    bp, sp, d = x_prompt.shape
    bs, ss, _ = x_sample.shape
    depth = norm1.shape[0]
    past = cache_da_k.shape[2]
    n_p = bp * sp
    rows = _Rows(n_p, bs, ss)
    tm = TOK_TILE
    assert sp % tm == 0 and ss % tm == 0 and n_p % ss == 0 and past == sp
    assert n_p % PEER_B_TILE == 0 and ss % PEER_B_TILE == 0

    da_heads = da_wv.shape[2] // DA_V_DIM
    mla_heads = mla_wo.shape[1] // MLA_V_DIM
    kv_lora = mla_kv_norm.shape[1]
    peer_heads = peer_keys.shape[1]

    x = jnp.concatenate([x_prompt.reshape(n_p, d), x_sample.reshape(bs * ss, d)], axis=0)

    n_cond = 1 + bs
    r_pad = -(-n_cond // 8) * 8
    cvecs = jnp.concatenate([c_ctx[None, :], c, jnp.zeros((r_pad - n_cond, d), F32)], axis=0)
    mods = _modulation(cvecs, ada_w, ada_b).reshape(depth, r_pad * 6, 1, d)

    da_cos, da_sin = _rope_tables(ss, tm, 0, DA_HEAD_DIM, DA_HEAD_DIM)
    mla_cos, mla_sin = _rope_tables(ss, tm, MLA_NOPE, MLA_ROPE, LANE)
    ident = ss // tm

    new_da_k, new_da_v, new_ckv, new_kpe = [], [], [], []
    for l in range(depth):
        modarr = mods[l]
        j = l // 2
        if l % 2 == 0:
            lam_init = 0.8 - 0.6 * math.exp(-0.3 * l)
            q, kb, vb, kf, vf = _da_qkv(
                x, rows, norm1[l][None, :], modarr,
                da_wq[j].astype(BF16), da_wk[j].astype(BF16), da_wv[j].astype(BF16),
                jnp.tile(da_q_norm[j][None, :], (1, 2)), jnp.tile(da_k_norm[j][None, :], (1, 2)), da_cos, da_sin)
            new_da_k.append(kf[:n_p].reshape(bp, sp, 2 * da_heads, DA_HEAD_DIM))
            new_da_v.append(vf[:n_p].reshape(bp, sp, da_heads, DA_V_DIM))
            lam_vecs = jnp.stack([da_lq1[j], da_lk1[j], da_lq2[j], da_lk2[j]]).astype(F32)
            extra = [lam_vecs, da_subln[j][None, :]]
            extra_specs = [pl.BlockSpec(lam_vecs.shape, lambda b, hp, qi: (0, 0)),
                           pl.BlockSpec((1, DA_V_DIM), lambda b, hp, qi: (0, 0))]
            ck = cache_da_k[:, j].reshape(bs * past, 2 * da_heads * DA_HEAD_DIM)
            cv = cache_da_v[:, j].reshape(bs * past, da_heads * DA_V_DIM)
            o_p = _attention(functools.partial(_da_attn_kernel, lam_init, 1), da_heads, LANE, q, sp, 0, bp,
                             [(kb, vb, sp, 0)], extra, extra_specs, "da_attn_prompt")
            o_s = _attention(functools.partial(_da_attn_kernel, lam_init, 2), da_heads, LANE, q, ss, n_p, bs,
                             [(kb, vb, ss, n_p), (ck, cv, past, 0)], extra, extra_specs, "da_attn_latent")
            wo = da_wo[j].astype(BF16)
        else:
            wuq = _pad_heads(mla_wuq[j], mla_heads, MLA_QK_DIM).astype(BF16)
            wdkv = jnp.concatenate([
                mla_wdkv[j][:, :kv_lora],
                jnp.pad(mla_wdkv[j][:, kv_lora:], ((0, 0), (MLA_NOPE, LANE - MLA_QK_DIM)))], axis=1).astype(BF16)
            wukv = mla_wukv[j].reshape(kv_lora, mla_heads, MLA_NOPE + MLA_V_DIM)
            wuk = jnp.pad(wukv[:, :, :MLA_NOPE], ((0, 0), (0, 0), (0, LANE - MLA_NOPE)))
            wuk = wuk.reshape(kv_lora, mla_heads * LANE).astype(BF16)
            wuv = wukv[:, :, MLA_NOPE:].reshape(kv_lora, mla_heads * MLA_V_DIM).astype(BF16)
            q, ckv, kpe = _mla_q(
                x, rows, norm1[l][None, :], modarr, mla_wdq[j].astype(BF16), mla_qa_norm[j][None, :], wuq,
                _pad_lanes(mla_q_norm[j]), wdkv, mla_kv_norm[j][None, :], mla_cos, mla_sin)
            new_ckv.append(ckv[:n_p].reshape(bp, sp, kv_lora))
            new_kpe.append(kpe[:n_p, MLA_NOPE:MLA_QK_DIM].reshape(bp, sp, MLA_ROPE))
            kn = _pad_lanes(mla_k_norm[j])
            k, v = _mla_kv(ckv, kpe, wuk, wuv, kn, mla_cos, mla_sin, lambda i: rows.pos_blk(i, tm))
            c_ckv = cache_mla_ckv[:, j].reshape(bs * past, kv_lora)
            c_kpe = jnp.pad(cache_mla_kpe[:, j].reshape(bs * past, MLA_ROPE), ((0, 0), (MLA_NOPE, LANE - MLA_QK_DIM)))
            ck, cv = _mla_kv(c_ckv, c_kpe, wuk, wuv, kn, mla_cos, mla_sin, lambda i: ident)
            o_p = _attention(functools.partial(_mla_attn_kernel, 1), mla_heads // 2, 2 * LANE, q, sp, 0, bp,
                             [(k, v, sp, 0)], [], [], "mla_attn_prompt")
            o_s = _attention(functools.partial(_mla_attn_kernel, 2), mla_heads // 2, 2 * LANE, q, ss, n_p, bs,
                             [(k, v, ss, n_p), (ck, cv, past, 0)], [], [], "mla_attn_latent")
            wo = mla_wo[j].astype(BF16)
        x = _out_proj(jnp.concatenate([o_p, o_s], axis=0), wo, x, rows, modarr, 2)

        wqt = peer_wq[l].T.astype(BF16)
        keys = peer_keys[l].reshape(2 * peer_heads, PEER_N_KEYS, PEER_HALF).astype(BF16)
        ht, e1, e0n, tau = _peer_a(x, rows, norm2[l][None, :], modarr, wqt, keys)
        x = _peer_b(ht, peer_u[l].astype(BF16), peer_v[l].T.astype(BF16), e1, e0n, tau, x, rows, modarr)

    return (x[:n_p].reshape(bp, sp, d), x[n_p:].reshape(bs, ss, d),
            jnp.stack(new_da_k, axis=1), jnp.stack(new_da_v, axis=1),
            jnp.stack(new_ckv, axis=1), jnp.stack(new_kpe, axis=1))
```

```python
import functools
import math

import jax
import jax.numpy as jnp
from jax import lax
from jax.experimental import pallas as pl
from jax.experimental.pallas import tpu as pltpu

GRID_W = 64
DA_HEAD_DIM = 64
DA_V_DIM = 2 * DA_HEAD_DIM
MLA_NOPE = 64
MLA_ROPE = 32
MLA_QK_DIM = MLA_NOPE + MLA_ROPE
MLA_V_DIM = 64
PEER_N_KEYS = 128
PEER_HALF = 128
PEER_TOPK = 16
ROPE_BASE = 10000.0
EPS = 1e-6

LANE = 128
VMEM_LIMIT = 56 * 1024 * 1024
TOK_TILE = 256
Q_TILE = 256
KEY_CHUNK = 4096
LOG2E = math.log2(math.e)
PEER_A_TILE = 256
PEER_B_TILE = 512
PEER_E_TILE = 2048
PEER_B_CHUNK = 256

BF16 = jnp.bfloat16
F32 = jnp.float32


def _params(sem, flags=None):
    return pltpu.CompilerParams(dimension_semantics=sem, vmem_limit_bytes=VMEM_LIMIT, flags=flags)


def _dot(a, b):
    return jnp.dot(a, b, preferred_element_type=F32)


def _dot_nt(a, b):
    return lax.dot_general(a, b, (((1,), (1,)), ((), ())), preferred_element_type=F32)


def _rms(x):
    return x * lax.rsqrt(jnp.mean(x * x, axis=-1, keepdims=True) + EPS)


def _modulate(x, g, shift, scale):
    return _rms(x) * g * (1.0 + scale) + shift


def _rope(x, cos, sin, half):
    lane = lax.broadcasted_iota(jnp.int32, x.shape, 1)
    first = (lane & (2 * half - 1)) < half
    partner = jnp.where(first, pltpu.roll(x, LANE - half, 1), pltpu.roll(x, half, 1))
    return x * cos + partner * sin


class _Rows:
    def __init__(self, n_p, n_b, s_lat):
        self.n_p, self.n_b, self.s_lat = n_p, n_b, s_lat
        self.n = n_p + n_b * s_lat

    def mod_row(self, i, tm):
        np_t = self.n_p // tm
        return jnp.where(i < np_t, 0, 1 + (i - np_t) // (self.s_lat // tm))

    def pos_blk(self, i, tm):
        np_t = self.n_p // tm
        return jnp.where(i < np_t, self.s_lat // tm, (i - np_t) % (self.s_lat // tm))


def _mod_spec(rows, tm, which, d):
    return pl.BlockSpec((1, 1, d), lambda i: (rows.mod_row(i, tm) * 6 + which, 0, 0))


def _rope_tables(s_lat, tm, rot_lo, rot_w, period):
    hd = rot_w // 2
    half = hd // 2
    lane = jnp.arange(LANE)
    d = (lane % period) - rot_lo
    rot = (d >= 0) & (d < rot_w)
    d = jnp.clip(d, 0, rot_w - 1)
    use_row = d < hd
    dd = d % hd
    i = dd % half
    first = dd < half
    inv = 1.0 / (ROPE_BASE ** ((2 * i).astype(F32) / hd))
    t = jnp.arange(s_lat)
    row = (t // GRID_W).astype(F32)
    col = (t % GRID_W).astype(F32)
    pos = jnp.where(use_row[None, :], row[:, None], col[:, None])
    ang = pos * inv[None, :]
    cos = jnp.where(rot[None, :], jnp.cos(ang), 1.0)
    sin = jnp.where(rot[None, :], jnp.where(first[None, :], -jnp.sin(ang), jnp.sin(ang)), 0.0)
    cos = jnp.concatenate([cos, jnp.ones((tm, LANE), F32)], axis=0)
    sin = jnp.concatenate([sin, jnp.zeros((tm, LANE), F32)], axis=0)
    return cos.astype(F32), sin.astype(F32)


def _mod_kernel(c_ref, w_ref, b_ref, o_ref):
    c = c_ref[...]
    s = c * jax.nn.sigmoid(c)
    o_ref[...] = _dot(s.astype(BF16), w_ref[...].astype(BF16)) + b_ref[...]


def _modulation(cvecs, ada_w, ada_b):
    depth, d, d6 = ada_w.shape
    r = cvecs.shape[0]
    tn = d
    return pl.pallas_call(
        _mod_kernel,
        grid=(depth, d6 // tn),
        in_specs=[
            pl.BlockSpec((r, d), lambda l, j: (0, 0)),
            pl.BlockSpec((None, d, tn), lambda l, j: (l, 0, j)),
            pl.BlockSpec((None, 1, tn), lambda l, j: (l, 0, j)),
        ],
        out_specs=pl.BlockSpec((None, r, tn), lambda l, j: (l, 0, j)),
        out_shape=jax.ShapeDtypeStruct((depth, r, d6), F32),
        compiler_params=_params(("arbitrary", "arbitrary")),
        name="adaln_modulation",
    )(cvecs, ada_w, ada_b.reshape(depth, 1, d6))


def _da_qkv_kernel(x_ref, n_ref, sh_ref, sc_ref, wq_ref, wk_ref, wv_ref, qn_ref, kn_ref, cos_ref, sin_ref,
                   q_ref, kb_ref, vb_ref, kf_ref, vf_ref):
    h = _modulate(x_ref[...], n_ref[...], sh_ref[0], sc_ref[0]).astype(BF16)
    q = _dot(h, wq_ref[...])
    k = _dot(h, wk_ref[...])
    v = _dot(h, wv_ref[...])
    vf_ref[...] = v
    vb_ref[...] = v.astype(BF16)
    cos = cos_ref[...]
    sin = sin_ref[...]
    lane = lax.broadcasted_iota(jnp.int32, (x_ref.shape[0], LANE), 1)
    lo = lane < DA_HEAD_DIM

    def head_norm(t, g):
        sq = t * t
        s_lo = jnp.sum(jnp.where(lo, sq, 0.0), axis=-1, keepdims=True)
        s_hi = jnp.sum(jnp.where(lo, 0.0, sq), axis=-1, keepdims=True)
        ms = jnp.where(lo, s_lo, s_hi) * (1.0 / DA_HEAD_DIM)
        return t * lax.rsqrt(ms + EPS) * g

    scale = DA_HEAD_DIM ** -0.5 * LOG2E
    for j in range(q.shape[1] // LANE):
        sl = slice(j * LANE, (j + 1) * LANE)
        qj = _rope(head_norm(q[:, sl], qn_ref[...]), cos, sin, DA_HEAD_DIM // 4)
        kj = _rope(head_norm(k[:, sl], kn_ref[...]), cos, sin, DA_HEAD_DIM // 4)
        q_ref[:, sl] = (qj * scale).astype(BF16)
        kf_ref[:, sl] = kj
        kb_ref[:, sl] = kj.astype(BF16)


def _da_qkv(x, rows, norm, modarr, wq, wk, wv, qn, kn, cos, sin):
    n, d = x.shape
    tm = TOK_TILE
    w = wq.shape[1]
    full = lambda shape: pl.BlockSpec(shape, lambda i: (0,) * len(shape))
    tok = lambda width: pl.BlockSpec((tm, width), lambda i: (i, 0))
    tab = pl.BlockSpec((tm, LANE), lambda i: (rows.pos_blk(i, tm), 0))
    return pl.pallas_call(
        _da_qkv_kernel,
        grid=(n // tm,),
        in_specs=[tok(d), full((1, d)), _mod_spec(rows, tm, 0, d), _mod_spec(rows, tm, 1, d),
                  full((d, w)), full((d, w)), full((d, w)), full((1, LANE)), full((1, LANE)), tab, tab],
        out_specs=[tok(w)] * 5,
        out_shape=[jax.ShapeDtypeStruct((n, w), BF16)] * 3 + [jax.ShapeDtypeStruct((n, w), F32)] * 2,
        compiler_params=_params(("arbitrary",)),
        name="da_qkv",
    )(x, norm, modarr, modarr, wq, wk, wv, qn, kn, cos, sin)


def _attend(qs, ksls, kv_refs):
    n = len(qs)
    m, den, acc = [None] * n, [None] * n, [None] * n
    for k_ref, v_ref in kv_refs:
        sk = k_ref.shape[0]
        step = min(KEY_CHUNK, sk)
        for c0 in range(0, sk, step):
            v = v_ref[c0:c0 + step, :].astype(BF16)
            for a in range(n):
                s = _dot_nt(qs[a], k_ref[c0:c0 + step, ksls[a]].astype(BF16))
                mx = s.max(axis=-1, keepdims=True)
                if m[a] is None:
                    m[a] = mx
                    p = jnp.exp2(s - mx)
                    den[a] = p.sum(axis=-1, keepdims=True)
                    acc[a] = _dot(p.astype(BF16), v)
                else:
                    m_new = jnp.maximum(m[a], mx)
                    alpha = jnp.exp2(m[a] - m_new)
                    p = jnp.exp2(s - m_new)
                    den[a] = alpha * den[a] + p.sum(axis=-1, keepdims=True)
                    acc[a] = alpha * acc[a] + _dot(p.astype(BF16), v)
                    m[a] = m_new
    return [acc[a] / den[a] for a in range(n)]


def _da_attn_kernel(lam_init, n_chunks, lam_ref, subln_ref, q_ref, *refs):
    kv = [(refs[2 * c], refs[2 * c + 1]) for c in range(n_chunks)]
    o_ref = refs[2 * n_chunks]
    lv = lam_ref[...]
    lam = (jnp.exp(jnp.sum(lv[0:1] * lv[1:2], axis=-1, keepdims=True))
           - jnp.exp(jnp.sum(lv[2:3] * lv[3:4], axis=-1, keepdims=True)) + lam_init)
    q = q_ref[...]
    lane = lax.broadcasted_iota(jnp.int32, q.shape, 1)
    zero = jnp.zeros_like(q)
    qs = [jnp.where(lane < DA_HEAD_DIM, q, zero), jnp.where(lane < DA_HEAD_DIM, zero, q)]
    o1, o2 = _attend(qs, [slice(None)] * 2, kv)
    o = o1 - lam * o2
    o = _rms(o) * subln_ref[...] * (1.0 - lam_init)
    o_ref[...] = o.astype(o_ref.dtype)


def _mla_attn_kernel(n_chunks, q_ref, *refs):
    kv = [(refs[2 * c], refs[2 * c + 1]) for c in range(n_chunks)]
    o_ref = refs[2 * n_chunks]
    sls = [slice(half * LANE, (half + 1) * LANE) for half in range(2)]
    outs = _attend([q_ref[:, sl] for sl in sls], sls, kv)
    lane = lax.broadcasted_iota(jnp.int32, outs[0].shape, 1)
    o_ref[...] = jnp.where(lane < MLA_V_DIM, outs[0], outs[1]).astype(o_ref.dtype)


def _attention(kernel, n_pairs, qw, q, seq, row_off, n_b, chunks, extra_in, extra_specs, name):
    tq = min(Q_TILE, seq)
    nq = seq // tq
    q_spec = pl.BlockSpec((tq, qw), lambda b, hp, qi: (row_off // tq + b * nq + qi, hp))
    in_specs = list(extra_specs) + [q_spec]
    args = list(extra_in) + [q]
    for k, v, sk, off in chunks:
        in_specs.append(pl.BlockSpec((sk, qw), functools.partial(lambda b, hp, qi, o, s: (o // s + b, hp), o=off, s=sk)))
        in_specs.append(pl.BlockSpec((sk, LANE), functools.partial(lambda b, hp, qi, o, s: (o // s + b, hp), o=off, s=sk)))
        args += [k, v]
    return pl.pallas_call(
        kernel,
        grid=(n_b, n_pairs, nq),
        in_specs=in_specs,
        out_specs=pl.BlockSpec((tq, LANE), lambda b, hp, qi: (b * nq + qi, hp)),
        out_shape=jax.ShapeDtypeStruct((n_b * seq, n_pairs * LANE), BF16),
        compiler_params=_params(("arbitrary", "arbitrary", "arbitrary")),
        name=name,
    )(*args)


def _out_proj_kernel(o_ref, w_ref, x_ref, g_ref, y_ref):
    y_ref[...] = x_ref[...] + g_ref[0] * _dot(o_ref[...], w_ref[...])


def _out_proj(o, w, x, rows, modarr, which):
    n, d = x.shape
    tm = TOK_TILE
    k = o.shape[1]
    return pl.pallas_call(
        _out_proj_kernel,
        grid=(n // tm,),
        in_specs=[pl.BlockSpec((tm, k), lambda i: (i, 0)), pl.BlockSpec((k, d), lambda i: (0, 0)),
                  pl.BlockSpec((tm, d), lambda i: (i, 0)), _mod_spec(rows, tm, which, d)],
        out_specs=pl.BlockSpec((tm, d), lambda i: (i, 0)),
        out_shape=jax.ShapeDtypeStruct((n, d), F32),
        compiler_params=_params(("arbitrary",)),
        name="out_proj",
    )(o, w, x, modarr)


def _head_norm128(t, g, width):
    ms = jnp.sum(t * t, axis=-1, keepdims=True) * (1.0 / width)
    return t * lax.rsqrt(ms + EPS) * g


def _mla_q_kernel(kv_lora, x_ref, n_ref, sh_ref, sc_ref, wdq_ref, qan_ref, wuq_ref, qn_ref, wdkv_ref, kvn_ref,
                  cos_ref, sin_ref, q_ref, ckv_ref, kpe_ref):
    h = _modulate(x_ref[...], n_ref[...], sh_ref[0], sc_ref[0]).astype(BF16)
    cq = (_rms(_dot(h, wdq_ref[...])) * qan_ref[...]).astype(BF16)
    q = _dot(cq, wuq_ref[...])
    cos = cos_ref[...]
    sin = sin_ref[...]
    scale = MLA_QK_DIM ** -0.5 * LOG2E
    for j in range(q.shape[1] // LANE):
        sl = slice(j * LANE, (j + 1) * LANE)
        qj = _rope(_head_norm128(q[:, sl], qn_ref[...], MLA_QK_DIM), cos, sin, MLA_ROPE // 4)
        q_ref[:, sl] = (qj * scale).astype(BF16)
    ckvf = _dot(h, wdkv_ref[...])
    ckv_ref[...] = _rms(ckvf[:, :kv_lora]) * kvn_ref[...]
    kpe_ref[...] = ckvf[:, kv_lora:]


def _mla_q(x, rows, norm, modarr, wdq, qan, wuq, qn, wdkv, kvn, cos, sin):
    n, d = x.shape
    tm = TOK_TILE
    ql = wdq.shape[1]
    qw = wuq.shape[1]
    kv_lora = kvn.shape[1]
    full = lambda shape: pl.BlockSpec(shape, lambda i: (0,) * len(shape))
    tok = lambda width: pl.BlockSpec((tm, width), lambda i: (i, 0))
    tab = pl.BlockSpec((tm, LANE), lambda i: (rows.pos_blk(i, tm), 0))
    return pl.pallas_call(
        functools.partial(_mla_q_kernel, kv_lora),
        grid=(n // tm,),
        in_specs=[tok(d), full((1, d)), _mod_spec(rows, tm, 0, d), _mod_spec(rows, tm, 1, d),
                  full((d, ql)), full((1, ql)), full((ql, qw)), full((1, LANE)),
                  full((d, kv_lora + LANE)), full((1, kv_lora)), tab, tab],
        out_specs=[tok(qw), tok(kv_lora), tok(LANE)],
        out_shape=[jax.ShapeDtypeStruct((n, qw), BF16), jax.ShapeDtypeStruct((n, kv_lora), F32),
                   jax.ShapeDtypeStruct((n, LANE), F32)],
        compiler_params=_params(("arbitrary",)),
        name="mla_q_ckv",
    )(x, norm, modarr, modarr, wdq, qan, wuq, qn, wdkv, kvn, cos, sin)


def _mla_kv_kernel(ckv_ref, kpe_ref, wuk_ref, wuv_ref, kn_ref, cos_ref, sin_ref, k_ref, v_ref):
    c = ckv_ref[...].astype(BF16)
    kn = _dot(c, wuk_ref[...])
    v_ref[...] = _dot(c, wuv_ref[...]).astype(BF16)
    kpe = kpe_ref[...]
    cos = cos_ref[...]
    sin = sin_ref[...]
    for j in range(kn.shape[1] // LANE):
        sl = slice(j * LANE, (j + 1) * LANE)
        kj = _rope(_head_norm128(kn[:, sl] + kpe, kn_ref[...], MLA_QK_DIM), cos, sin, MLA_ROPE // 4)
        k_ref[:, sl] = kj.astype(BF16)


def _mla_kv(ckv, kpe, wuk, wuv, kn, cos, sin, pos_blk):
    n, kv_lora = ckv.shape
    tm = TOK_TILE
    kw = wuk.shape[1]
    vw = wuv.shape[1]
    full = lambda shape: pl.BlockSpec(shape, lambda i: (0,) * len(shape))
    tok = lambda width: pl.BlockSpec((tm, width), lambda i: (i, 0))
    tab = pl.BlockSpec((tm, LANE), lambda i: (pos_blk(i), 0))
    return pl.pallas_call(
        _mla_kv_kernel,
        grid=(n // tm,),
        in_specs=[tok(kv_lora), tok(LANE), full((kv_lora, kw)), full((kv_lora, vw)), full((1, LANE)), tab, tab],
        out_specs=[tok(kw), tok(vw)],
        out_shape=[jax.ShapeDtypeStruct((n, kw), BF16), jax.ShapeDtypeStruct((n, vw), BF16)],
        compiler_params=_params(("arbitrary",)),
        name="mla_kv_expand",
    )(ckv, kpe, wuk, wuv, kn, cos, sin)


def _sort_pairs(n):
    pairs = []
    p = 1
    while p < n:
        k = p
        while k >= 1:
            for j in range(k % p, n - k, 2 * k):
                for i in range(min(k, n - j - k)):
                    if (i + j) // (2 * p) == (i + j + k) // (2 * p):
                        pairs.append((i + j, i + j + k))
            k //= 2
        p *= 2
    return pairs


def _cmpx(x, i, j):
    a, b = x[i], x[j]
    if b is None:
        return
    if a is None:
        x[i], x[j] = b, None
        return
    x[i], x[j] = jnp.maximum(a, b), jnp.minimum(a, b)


def _top_sorted(x):
    k = PEER_TOPK
    x = list(x)
    for i, j in _sort_pairs(k):
        _cmpx(x, i, j)
    shift = 4
    while shift >= 1:
        y = []
        for i in range(k):
            other = x[k - 1 - i]
            other = None if other is None else pltpu.roll(other, 8 - shift, 0)
            if x[i] is None:
                y.append(other)
            elif other is None:
                y.append(x[i])
            else:
                y.append(jnp.maximum(x[i], other))
        d = k // 2
        while d >= 1:
            for i in range(k):
                if (i & d) == 0:
                    _cmpx(y, i, i + d)
            d //= 2
        x = y
        shift //= 2
    return x


def _pair_candidates(a_ref, b_ref, scale):
    k = PEER_TOPK
    b_lo = b_ref[0:8, :]
    sub = lax.broadcasted_iota(jnp.int32, b_lo.shape, 0)
    row = lambda r: a_ref[r:r + 1, :] * scale
    out = [row(0) * b_lo, row(0) * b_ref[8:16, :]]
    for r in range(1, 8):
        n_r = k // (r + 1)
        c = row(r) * b_lo
        out.append(c if n_r >= 8 else jnp.where(sub < n_r, c, -1.0))
    out.append((a_ref[8:16, :] * scale) * b_ref[0:1, :])
    return out + [None] * (k - len(out))


def _peer_a_kernel(n_heads, x_ref, n_ref, sh_ref, sc_ref, wqt_ref, keys_ref,
                   ht_ref, e1_ref, e0n_ref, tau_ref, qt_sc, e_sc, top_sc):
    k = PEER_TOPK
    h = _modulate(x_ref[...], n_ref[...], sh_ref[0], sc_ref[0])
    ht = h.T.astype(BF16)
    ht_ref[...] = ht
    qt_sc[...] = _dot(wqt_ref[...], ht)

    def per_set(hc, carry):
        qb = qt_sc[pl.ds(pl.multiple_of(hc * PEER_HALF, PEER_HALF), PEER_HALF), :]
        s = _dot(keys_ref[hc], qb.astype(BF16))
        e = jnp.exp(s - jnp.max(s, axis=0, keepdims=True))
        e_sc[hc] = e
        top = _top_sorted([e[g * 8:(g + 1) * 8, :] for g in range(PEER_N_KEYS // 8)])
        for r in range(k):
            top_sc[hc, r:r + 1, :] = top[r][0:1, :]
        return carry

    lax.fori_loop(0, 2 * n_heads, per_set, 0)

    def per_head(hh, carry):
        a_ref = top_sc.at[2 * hh]
        b_ref = top_sc.at[2 * hh + 1]
        top = _top_sorted(_pair_candidates(a_ref, b_ref, 1.0))
        z = top[0][0:1, :]
        for r in range(1, k):
            z = z + top[r][0:1, :]
        rz = 1.0 / z
        e0n_ref[hh] = e_sc[2 * hh] * rz
        e1_ref[hh] = e_sc[2 * hh + 1]
        topn = _top_sorted(_pair_candidates(a_ref, b_ref, rz))
        tau_ref[pl.ds(hh, 1), :] = topn[k - 1][0:1, :]
        return carry

    lax.fori_loop(0, n_heads, per_head, 0)


def _peer_a(x, rows, norm, modarr, wqt, keys):
    n, d = x.shape
    t = PEER_A_TILE
    n_sets = keys.shape[0]
    n_heads = n_sets // 2
    full = lambda shape: pl.BlockSpec(shape, lambda i: (0,) * len(shape))
    return pl.pallas_call(
        functools.partial(_peer_a_kernel, n_heads),
        grid=(n // t,),
        in_specs=[pl.BlockSpec((t, d), lambda i: (i, 0)), full((1, d)),
                  _mod_spec(rows, t, 3, d), _mod_spec(rows, t, 4, d),
                  full(wqt.shape), full(keys.shape)],
        out_specs=[pl.BlockSpec((d, t), lambda i: (0, i)),
                   pl.BlockSpec((n_heads, PEER_N_KEYS, t), lambda i: (0, 0, i)),
                   pl.BlockSpec((n_heads, PEER_N_KEYS, t), lambda i: (0, 0, i)),
                   pl.BlockSpec((n_heads, t), lambda i: (0, i))],
        out_shape=[jax.ShapeDtypeStruct((d, n), BF16),
                   jax.ShapeDtypeStruct((n_heads, PEER_N_KEYS, n), F32),
                   jax.ShapeDtypeStruct((n_heads, PEER_N_KEYS, n), F32),
                   jax.ShapeDtypeStruct((n_heads, n), F32)],
        scratch_shapes=[pltpu.VMEM((n_sets * PEER_HALF, t), F32),
                        pltpu.VMEM((n_sets, PEER_N_KEYS, t), F32),
                        pltpu.VMEM((n_sets, PEER_TOPK, t), F32)],
        compiler_params=_params(("arbitrary",)),
        name="peer_select",
    )(x, norm, modarr, modarr, wqt, keys)


def _peer_b_kernel(n_heads, n_chunks, ht_ref, u_ref, vt_ref, e1_ref, e0n_ref, tau_ref, x_ref, g_ref, y_ref,
                   acc_sc, *chunk_sc):
    at_sc, w_sc = chunk_sc[:n_chunks], chunk_sc[n_chunks:]
    e = pl.program_id(1)

    @pl.when(e == 0)
    def _():
        acc_sc[...] = jnp.zeros_like(acc_sc)

    ht = ht_ref[...]
    t = ht.shape[1]
    ch = PEER_B_CHUNK
    for c in range(n_chunks):
        at_sc[c][...] = _dot(u_ref[c * ch:(c + 1) * ch, :], ht)
    for c in range(n_chunks):
        for r0 in range(0, ch, 64):
            ii, j0 = divmod(c * ch + r0, PEER_N_KEYS)
            for l0 in range(0, t, LANE):
                ls = slice(l0, l0 + LANE)
                gate = None
                for hh in range(n_heads):
                    p = e1_ref[hh, j0:j0 + 64, ls] * e0n_ref[hh, ii:ii + 1, ls]
                    g = jnp.where(p >= tau_ref[hh:hh + 1, ls], p, 0.0)
                    gate = g if gate is None else gate + g
                w_sc[c][r0:r0 + 64, ls] = (gate * jax.nn.gelu(at_sc[c][r0:r0 + 64, ls])).astype(BF16)
        acc_sc[...] += _dot(vt_ref[:, c * ch:(c + 1) * ch], w_sc[c][...])

    @pl.when(e == pl.num_programs(1) - 1)
    def _():
        y_ref[...] = x_ref[...] + g_ref[0] * acc_sc[...].T


def _peer_b(ht, u, vt, e1, e0n, tau, x, rows, modarr):
    n, d = x.shape
    t = PEER_B_TILE
    te = PEER_E_TILE
    n_exp = u.shape[0]
    n_heads = e1.shape[0]
    ti = te // PEER_N_KEYS
    n_chunks = te // PEER_B_CHUNK
    return pl.pallas_call(
        functools.partial(_peer_b_kernel, n_heads, n_chunks),
        grid=(n // t, n_exp // te),
        in_specs=[pl.BlockSpec((d, t), lambda i, e: (0, i)),
                  pl.BlockSpec((te, d), lambda i, e: (e, 0)),
                  pl.BlockSpec((d, te), lambda i, e: (0, e)),
                  pl.BlockSpec((n_heads, PEER_N_KEYS, t), lambda i, e: (0, 0, i)),
                  pl.BlockSpec((n_heads, ti, t), lambda i, e: (0, e, i)),
                  pl.BlockSpec((n_heads, t), lambda i, e: (0, i)),
                  pl.BlockSpec((t, d), lambda i, e: (i, 0)),
                  pl.BlockSpec((1, 1, d), lambda i, e: (rows.mod_row(i, t) * 6 + 5, 0, 0))],
        out_specs=pl.BlockSpec((t, d), lambda i, e: (i, 0)),
        out_shape=jax.ShapeDtypeStruct((n, d), F32),
        scratch_shapes=([pltpu.VMEM((d, t), F32)] + [pltpu.VMEM((PEER_B_CHUNK, t), F32)] * n_chunks
                        + [pltpu.VMEM((PEER_B_CHUNK, t), BF16)] * n_chunks),
        compiler_params=_params(("arbitrary", "arbitrary")),
        name="peer_experts",
    )(ht, u, vt, e1, e0n, tau, x, modarr)


def _pad_heads(w, n_heads, width):
    k = w.shape[0]
    w = w.reshape(k, n_heads, width)
    return jnp.pad(w, ((0, 0), (0, 0), (0, LANE - width))).reshape(k, n_heads * LANE)


def _pad_lanes(g, lo=0):
    g = g.reshape(1, -1).astype(F32)
    return jnp.pad(g, ((0, 0), (lo, LANE - lo - g.shape[1])))


def kernel(x_prompt, x_sample, cache_da_k, cache_da_v, cache_mla_ckv, cache_mla_kpe, c, c_ctx, norm1, norm2, ada_w, ada_b, da_wq, da_wk, da_wv, da_wo, da_q_norm, da_k_norm, da_lq1, da_lk1, da_lq2, da_lk2, da_subln, mla_wdq, mla_qa_norm, mla_wuq, mla_wdkv, mla_kv_norm, mla_wukv, mla_q_norm, mla_k_norm, mla_wo, peer_wq, peer_keys, peer_u, peer_v):
    bp, sp, d = x_prompt.shape
    bs, ss, _ = x_sample.shape
    depth = norm1.shape[0]
    past = cache_da_k.shape[2]
    n_p = bp * sp
    rows = _Rows(n_p, bs, ss)
    tm = TOK_TILE
    assert sp % tm == 0 and ss % tm == 0 and n_p % ss == 0 and past == sp
    assert n_p % PEER_B_TILE == 0 and ss % PEER_B_TILE == 0

    da_heads = da_wv.shape[2] // DA_V_DIM
    mla_heads = mla_wo.shape[1] // MLA_V_DIM
    kv_lora = mla_kv_norm.shape[1]
    peer_heads = peer_keys.shape[1]

    x = jnp.concatenate([x_prompt.reshape(n_p, d), x_sample.reshape(bs * ss, d)], axis=0)

    n_cond = 1 + bs
    r_pad = -(-n_cond // 8) * 8
    cvecs = jnp.concatenate([c_ctx[None, :], c, jnp.zeros((r_pad - n_cond, d), F32)], axis=0)
    mods = _modulation(cvecs, ada_w, ada_b).reshape(depth, r_pad * 6, 1, d)

    da_cos, da_sin = _rope_tables(ss, tm, 0, DA_HEAD_DIM, DA_HEAD_DIM)
    mla_cos, mla_sin = _rope_tables(ss, tm, MLA_NOPE, MLA_ROPE, LANE)
    ident = ss // tm

    new_da_k, new_da_v, new_ckv, new_kpe = [], [], [], []
    for l in range(depth):
        modarr = mods[l]
        j = l // 2
        if l % 2 == 0:
            lam_init = 0.8 - 0.6 * math.exp(-0.3 * l)
            q, kb, vb, kf, vf = _da_qkv(
                x, rows, norm1[l][None, :], modarr,
                da_wq[j].astype(BF16), da_wk[j].astype(BF16), da_wv[j].astype(BF16),
                jnp.tile(da_q_norm[j][None, :], (1, 2)), jnp.tile(da_k_norm[j][None, :], (1, 2)), da_cos, da_sin)
            new_da_k.append(kf[:n_p].reshape(bp, sp, 2 * da_heads, DA_HEAD_DIM))
            new_da_v.append(vf[:n_p].reshape(bp, sp, da_heads, DA_V_DIM))
            lam_vecs = jnp.stack([da_lq1[j], da_lk1[j], da_lq2[j], da_lk2[j]]).astype(F32)
            extra = [lam_vecs, da_subln[j][None, :]]
            extra_specs = [pl.BlockSpec(lam_vecs.shape, lambda b, hp, qi: (0, 0)),
                           pl.BlockSpec((1, DA_V_DIM), lambda b, hp, qi: (0, 0))]
            ck = cache_da_k[:, j].reshape(bs * past, 2 * da_heads * DA_HEAD_DIM)
            cv = cache_da_v[:, j].reshape(bs * past, da_heads * DA_V_DIM)
            o_p = _attention(functools.partial(_da_attn_kernel, lam_init, 1), da_heads, LANE, q, sp, 0, bp,
                             [(kb, vb, sp, 0)], extra, extra_specs, "da_attn_prompt")
            o_s = _attention(functools.partial(_da_attn_kernel, lam_init, 2), da_heads, LANE, q, ss, n_p, bs,
                             [(kb, vb, ss, n_p), (ck, cv, past, 0)], extra, extra_specs, "da_attn_latent")
            wo = da_wo[j].astype(BF16)
        else:
            wuq = _pad_heads(mla_wuq[j], mla_heads, MLA_QK_DIM).astype(BF16)
            wdkv = jnp.concatenate([
                mla_wdkv[j][:, :kv_lora],
                jnp.pad(mla_wdkv[j][:, kv_lora:], ((0, 0), (MLA_NOPE, LANE - MLA_QK_DIM)))], axis=1).astype(BF16)
            wukv = mla_wukv[j].reshape(kv_lora, mla_heads, MLA_NOPE + MLA_V_DIM)
            wuk = jnp.pad(wukv[:, :, :MLA_NOPE], ((0, 0), (0, 0), (0, LANE - MLA_NOPE)))
            wuk = wuk.reshape(kv_lora, mla_heads * LANE).astype(BF16)
            wuv = wukv[:, :, MLA_NOPE:].reshape(kv_lora, mla_heads * MLA_V_DIM).astype(BF16)
            q, ckv, kpe = _mla_q(
                x, rows, norm1[l][None, :], modarr, mla_wdq[j].astype(BF16), mla_qa_norm[j][None, :], wuq,
                _pad_lanes(mla_q_norm[j]), wdkv, mla_kv_norm[j][None, :], mla_cos, mla_sin)
            new_ckv.append(ckv[:n_p].reshape(bp, sp, kv_lora))
            new_kpe.append(kpe[:n_p, MLA_NOPE:MLA_QK_DIM].reshape(bp, sp, MLA_ROPE))
            kn = _pad_lanes(mla_k_norm[j])
            k, v = _mla_kv(ckv, kpe, wuk, wuv, kn, mla_cos, mla_sin, lambda i: rows.pos_blk(i, tm))
            c_ckv = cache_mla_ckv[:, j].reshape(bs * past, kv_lora)
            c_kpe = jnp.pad(cache_mla_kpe[:, j].reshape(bs * past, MLA_ROPE), ((0, 0), (MLA_NOPE, LANE - MLA_QK_DIM)))
            ck, cv = _mla_kv(c_ckv, c_kpe, wuk, wuv, kn, mla_cos, mla_sin, lambda i: ident)
            o_p = _attention(functools.partial(_mla_attn_kernel, 1), mla_heads // 2, 2 * LANE, q, sp, 0, bp,
                             [(k, v, sp, 0)], [], [], "mla_attn_prompt")
            o_s = _attention(functools.partial(_mla_attn_kernel, 2), mla_heads // 2, 2 * LANE, q, ss, n_p, bs,
                             [(k, v, ss, n_p), (ck, cv, past, 0)], [], [], "mla_attn_latent")
            wo = mla_wo[j].astype(BF16)
        x = _out_proj(jnp.concatenate([o_p, o_s], axis=0), wo, x, rows, modarr, 2)

        wqt = peer_wq[l].T.astype(BF16)
        keys = peer_keys[l].reshape(2 * peer_heads, PEER_N_KEYS, PEER_HALF).astype(BF16)
        ht, e1, e0n, tau = _peer_a(x, rows, norm2[l][None, :], modarr, wqt, keys)
        x = _peer_b(ht, peer_u[l].astype(BF16), peer_v[l].T.astype(BF16), e1, e0n, tau, x, rows, modarr)

    return (x[:n_p].reshape(bp, sp, d), x[n_p:].reshape(bs, ss, d),
            jnp.stack(new_da_k, axis=1), jnp.stack(new_da_v, axis=1),
            jnp.stack(new_ckv, axis=1), jnp.stack(new_kpe, axis=1))
```

```python
import functools
import math

import jax
import jax.numpy as jnp
from jax import lax
from jax.experimental import pallas as pl
from jax.experimental.pallas import tpu as pltpu

GRID_W = 64
DA_HEAD_DIM = 64
DA_V_DIM = 2 * DA_HEAD_DIM
MLA_NOPE = 64
MLA_ROPE = 32
MLA_QK_DIM = MLA_NOPE + MLA_ROPE
MLA_V_DIM = 64
PEER_N_KEYS = 128
PEER_HALF = 128
PEER_TOPK = 16
ROPE_BASE = 10000.0
EPS = 1e-6

LANE = 128
VMEM_LIMIT = 56 * 1024 * 1024
TOK_TILE = 256
Q_TILE = 1024
KEY_CHUNK = 1024
LOG2E = math.log2(math.e)
PEER_A_TILE = 256
PEER_B_TILE = 512
PEER_E_TILE = 2048
PEER_B_CHUNK = 256

BF16 = jnp.bfloat16
F32 = jnp.float32


def _params(sem, flags=None):
    return pltpu.CompilerParams(dimension_semantics=sem, vmem_limit_bytes=VMEM_LIMIT, flags=flags)


def _dot(a, b):
    return jnp.dot(a, b, preferred_element_type=F32)


def _dot_nt(a, b):
    return lax.dot_general(a, b, (((1,), (1,)), ((), ())), preferred_element_type=F32)


def _rms(x):
    return x * lax.rsqrt(jnp.mean(x * x, axis=-1, keepdims=True) + EPS)


def _modulate(x, g, shift, scale):
    return _rms(x) * g * (1.0 + scale) + shift


def _rope(x, cos, sin, half):
    lane = lax.broadcasted_iota(jnp.int32, x.shape, 1)
    first = (lane & (2 * half - 1)) < half
    partner = jnp.where(first, pltpu.roll(x, LANE - half, 1), pltpu.roll(x, half, 1))
    return x * cos + partner * sin


class _Rows:
    def __init__(self, n_p, n_b, s_lat):
        self.n_p, self.n_b, self.s_lat = n_p, n_b, s_lat
        self.n = n_p + n_b * s_lat

    def mod_row(self, i, tm):
        np_t = self.n_p // tm
        return jnp.where(i < np_t, 0, 1 + (i - np_t) // (self.s_lat // tm))

    def pos_blk(self, i, tm):
        np_t = self.n_p // tm
        return jnp.where(i < np_t, self.s_lat // tm, (i - np_t) % (self.s_lat // tm))


def _mod_spec(rows, tm, which, d):
    return pl.BlockSpec((1, 1, d), lambda i: (rows.mod_row(i, tm) * 6 + which, 0, 0))


def _rope_tables(s_lat, tm, rot_lo, rot_w, period):
    hd = rot_w // 2
    half = hd // 2
    lane = jnp.arange(LANE)
    d = (lane % period) - rot_lo
    rot = (d >= 0) & (d < rot_w)
    d = jnp.clip(d, 0, rot_w - 1)
    use_row = d < hd
    dd = d % hd
    i = dd % half
    first = dd < half
    inv = 1.0 / (ROPE_BASE ** ((2 * i).astype(F32) / hd))
    t = jnp.arange(s_lat)
    row = (t // GRID_W).astype(F32)
    col = (t % GRID_W).astype(F32)
    pos = jnp.where(use_row[None, :], row[:, None], col[:, None])
    ang = pos * inv[None, :]
    cos = jnp.where(rot[None, :], jnp.cos(ang), 1.0)
    sin = jnp.where(rot[None, :], jnp.where(first[None, :], -jnp.sin(ang), jnp.sin(ang)), 0.0)
    cos = jnp.concatenate([cos, jnp.ones((tm, LANE), F32)], axis=0)
    sin = jnp.concatenate([sin, jnp.zeros((tm, LANE), F32)], axis=0)
    return cos.astype(F32), sin.astype(F32)


def _mod_kernel(c_ref, w_ref, b_ref, o_ref):
    c = c_ref[...]
    s = c * jax.nn.sigmoid(c)
    o_ref[...] = _dot(s.astype(BF16), w_ref[...].astype(BF16)) + b_ref[...]


def _modulation(cvecs, ada_w, ada_b):
    depth, d, d6 = ada_w.shape
    r = cvecs.shape[0]
    tn = d
    return pl.pallas_call(
        _mod_kernel,
        grid=(depth, d6 // tn),
        in_specs=[
            pl.BlockSpec((r, d), lambda l, j: (0, 0)),
            pl.BlockSpec((None, d, tn), lambda l, j: (l, 0, j)),
            pl.BlockSpec((None, 1, tn), lambda l, j: (l, 0, j)),
        ],
        out_specs=pl.BlockSpec((None, r, tn), lambda l, j: (l, 0, j)),
        out_shape=jax.ShapeDtypeStruct((depth, r, d6), F32),
        compiler_params=_params(("arbitrary", "arbitrary")),
        name="adaln_modulation",
    )(cvecs, ada_w, ada_b.reshape(depth, 1, d6))


def _da_qkv_kernel(x_ref, n_ref, sh_ref, sc_ref, wq_ref, wk_ref, wv_ref, qn_ref, kn_ref, cos_ref, sin_ref,
                   q_ref, kb_ref, vb_ref, kf_ref, vf_ref):
    h = _modulate(x_ref[...], n_ref[...], sh_ref[0], sc_ref[0]).astype(BF16)
    q = _dot(h, wq_ref[...])
    k = _dot(h, wk_ref[...])
    v = _dot(h, wv_ref[...])
    vf_ref[...] = v
    vb_ref[...] = v.astype(BF16)
    cos = cos_ref[...]
    sin = sin_ref[...]
    lane = lax.broadcasted_iota(jnp.int32, (x_ref.shape[0], LANE), 1)
    lo = lane < DA_HEAD_DIM

    def head_norm(t, g):
        sq = t * t
        s_lo = jnp.sum(jnp.where(lo, sq, 0.0), axis=-1, keepdims=True)
        s_hi = jnp.sum(jnp.where(lo, 0.0, sq), axis=-1, keepdims=True)
        ms = jnp.where(lo, s_lo, s_hi) * (1.0 / DA_HEAD_DIM)
        return t * lax.rsqrt(ms + EPS) * g

    scale = DA_HEAD_DIM ** -0.5 * LOG2E
    for j in range(q.shape[1] // LANE):
        sl = slice(j * LANE, (j + 1) * LANE)
        qj = _rope(head_norm(q[:, sl], qn_ref[...]), cos, sin, DA_HEAD_DIM // 4)
        kj = _rope(head_norm(k[:, sl], kn_ref[...]), cos, sin, DA_HEAD_DIM // 4)
        q_ref[:, sl] = (qj * scale).astype(BF16)
        kf_ref[:, sl] = kj
        kb_ref[:, sl] = kj.astype(BF16)


def _da_qkv(x, rows, norm, modarr, wq, wk, wv, qn, kn, cos, sin):
    n, d = x.shape
    tm = TOK_TILE
    w = wq.shape[1]
    full = lambda shape: pl.BlockSpec(shape, lambda i: (0,) * len(shape))
    tok = lambda width: pl.BlockSpec((tm, width), lambda i: (i, 0))
    tab = pl.BlockSpec((tm, LANE), lambda i: (rows.pos_blk(i, tm), 0))
    return pl.pallas_call(
        _da_qkv_kernel,
        grid=(n // tm,),
        in_specs=[tok(d), full((1, d)), _mod_spec(rows, tm, 0, d), _mod_spec(rows, tm, 1, d),
                  full((d, w)), full((d, w)), full((d, w)), full((1, LANE)), full((1, LANE)), tab, tab],
        out_specs=[tok(w)] * 5,
        out_shape=[jax.ShapeDtypeStruct((n, w), BF16)] * 3 + [jax.ShapeDtypeStruct((n, w), F32)] * 2,
        compiler_params=_params(("arbitrary",)),
        name="da_qkv",
    )(x, norm, modarr, modarr, wq, wk, wv, qn, kn, cos, sin)


def _attend(qs, ksls, kv_refs):
    n = len(qs)
    m, acc = [None] * n, [None] * n
    for k_ref, v_ref in kv_refs:
        sk = k_ref.shape[0]
        step = min(KEY_CHUNK, sk)
        for c0 in range(0, sk, step):
            v = v_ref[c0:c0 + step, :].astype(BF16)
            v1 = jnp.concatenate([v, jnp.ones_like(v)], axis=1)
            for a in range(n):
                s = _dot_nt(qs[a], k_ref[c0:c0 + step, ksls[a]].astype(BF16))
                mx = s.max(axis=-1, keepdims=True)
                if m[a] is None:
                    m[a] = mx
                    acc[a] = _dot(jnp.exp2((s - mx).astype(BF16)), v1)
                else:
                    m_new = jnp.maximum(m[a], mx)
                    alpha = jnp.exp2(m[a] - m_new)
                    acc[a] = alpha * acc[a] + _dot(jnp.exp2((s - m_new).astype(BF16)), v1)
                    m[a] = m_new
    return [acc[a][:, :LANE] / acc[a][:, LANE:] for a in range(n)]


def _da_attn_kernel(lam_init, n_chunks, lam_ref, subln_ref, q_ref, *refs):
    kv = [(refs[2 * c], refs[2 * c + 1]) for c in range(n_chunks)]
    o_ref = refs[2 * n_chunks]
    lv = lam_ref[...]
    lam = (jnp.exp(jnp.sum(lv[0:1] * lv[1:2], axis=-1, keepdims=True))
           - jnp.exp(jnp.sum(lv[2:3] * lv[3:4], axis=-1, keepdims=True)) + lam_init)
    q = q_ref[...]
    lane = lax.broadcasted_iota(jnp.int32, q.shape, 1)
    zero = jnp.zeros_like(q)
    qs = [jnp.where(lane < DA_HEAD_DIM, q, zero), jnp.where(lane < DA_HEAD_DIM, zero, q)]
    o1, o2 = _attend(qs, [slice(None)] * 2, kv)
    o = o1 - lam * o2
    o = _rms(o) * subln_ref[...] * (1.0 - lam_init)
    o_ref[...] = o.astype(o_ref.dtype)


def _mla_attn_kernel(n_chunks, q_ref, *refs):
    kv = [(refs[2 * c], refs[2 * c + 1]) for c in range(n_chunks)]
    o_ref = refs[2 * n_chunks]
    sls = [slice(half * LANE, (half + 1) * LANE) for half in range(2)]
    outs = _attend([q_ref[:, sl] for sl in sls], sls, kv)
    lane = lax.broadcasted_iota(jnp.int32, outs[0].shape, 1)
    o_ref[...] = jnp.where(lane < MLA_V_DIM, outs[0], outs[1]).astype(o_ref.dtype)


def _attention(kernel, n_pairs, qw, q, seq, row_off, n_b, chunks, extra_in, extra_specs, name):
    tq = min(Q_TILE, seq)
    nq = seq // tq
    q_spec = pl.BlockSpec((tq, qw), lambda b, hp, qi: (row_off // tq + b * nq + qi, hp))
    in_specs = list(extra_specs) + [q_spec]
    args = list(extra_in) + [q]
    for k, v, sk, off in chunks:
        in_specs.append(pl.BlockSpec((sk, qw), functools.partial(lambda b, hp, qi, o, s: (o // s + b, hp), o=off, s=sk)))
        in_specs.append(pl.BlockSpec((sk, LANE), functools.partial(lambda b, hp, qi, o, s: (o // s + b, hp), o=off, s=sk)))
        args += [k, v]
    return pl.pallas_call(
        kernel,
        grid=(n_b, n_pairs, nq),
        in_specs=in_specs,
        out_specs=pl.BlockSpec((tq, LANE), lambda b, hp, qi: (b * nq + qi, hp)),
        out_shape=jax.ShapeDtypeStruct((n_b * seq, n_pairs * LANE), BF16),
        compiler_params=_params(("arbitrary", "arbitrary", "arbitrary")),
        name=name,
    )(*args)


def _out_proj_kernel(n_prompt_tiles, op_ref, os_ref, w_ref, x_ref, g_ref, y_ref):
    o = jnp.where(pl.program_id(0) < n_prompt_tiles, op_ref[...], os_ref[...])
    y_ref[...] = x_ref[...] + g_ref[0] * _dot(o, w_ref[...])


def _out_proj(o_p, o_s, w, x, rows, modarr, which):
    n, d = x.shape
    tm = TOK_TILE
    k = o_p.shape[1]
    np_t = rows.n_p // tm
    return pl.pallas_call(
        functools.partial(_out_proj_kernel, np_t),
        grid=(n // tm,),
        in_specs=[pl.BlockSpec((tm, k), lambda i: (jnp.minimum(i, np_t - 1), 0)),
                  pl.BlockSpec((tm, k), lambda i: (jnp.maximum(i - np_t, 0), 0)),
                  pl.BlockSpec((k, d), lambda i: (0, 0)),
                  pl.BlockSpec((tm, d), lambda i: (i, 0)), _mod_spec(rows, tm, which, d)],
        out_specs=pl.BlockSpec((tm, d), lambda i: (i, 0)),
        out_shape=jax.ShapeDtypeStruct((n, d), F32),
        compiler_params=_params(("arbitrary",)),
        name="out_proj",
    )(o_p, o_s, w, x, modarr)


def _head_norm128(t, g, width):
    ms = jnp.sum(t * t, axis=-1, keepdims=True) * (1.0 / width)
    return t * lax.rsqrt(ms + EPS) * g


def _mla_q_kernel(kv_lora, x_ref, n_ref, sh_ref, sc_ref, wdq_ref, qan_ref, wuq_ref, qn_ref, wdkv_ref, kvn_ref,
                  cos_ref, sin_ref, q_ref, ckv_ref, kpe_ref):
    h = _modulate(x_ref[...], n_ref[...], sh_ref[0], sc_ref[0]).astype(BF16)
    cq = (_rms(_dot(h, wdq_ref[...])) * qan_ref[...]).astype(BF16)
    q = _dot(cq, wuq_ref[...])
    cos = cos_ref[...]
    sin = sin_ref[...]
    scale = MLA_QK_DIM ** -0.5 * LOG2E
    for j in range(q.shape[1] // LANE):
        sl = slice(j * LANE, (j + 1) * LANE)
        qj = _rope(_head_norm128(q[:, sl], qn_ref[...], MLA_QK_DIM), cos, sin, MLA_ROPE // 4)
        q_ref[:, sl] = (qj * scale).astype(BF16)
    ckvf = _dot(h, wdkv_ref[...])
    ckv_ref[...] = _rms(ckvf[:, :kv_lora]) * kvn_ref[...]
    kpe_ref[...] = ckvf[:, kv_lora:]


def _mla_q(x, rows, norm, modarr, wdq, qan, wuq, qn, wdkv, kvn, cos, sin):
    n, d = x.shape
    tm = TOK_TILE
    ql = wdq.shape[1]
    qw = wuq.shape[1]
    kv_lora = kvn.shape[1]
    full = lambda shape: pl.BlockSpec(shape, lambda i: (0,) * len(shape))
    tok = lambda width: pl.BlockSpec((tm, width), lambda i: (i, 0))
    tab = pl.BlockSpec((tm, LANE), lambda i: (rows.pos_blk(i, tm), 0))
    return pl.pallas_call(
        functools.partial(_mla_q_kernel, kv_lora),
        grid=(n // tm,),
        in_specs=[tok(d), full((1, d)), _mod_spec(rows, tm, 0, d), _mod_spec(rows, tm, 1, d),
                  full((d, ql)), full((1, ql)), full((ql, qw)), full((1, LANE)),
                  full((d, kv_lora + LANE)), full((1, kv_lora)), tab, tab],
        out_specs=[tok(qw), tok(kv_lora), tok(LANE)],
        out_shape=[jax.ShapeDtypeStruct((n, qw), BF16), jax.ShapeDtypeStruct((n, kv_lora), F32),
                   jax.ShapeDtypeStruct((n, LANE), F32)],
        compiler_params=_params(("arbitrary",)),
        name="mla_q_ckv",
    )(x, norm, modarr, modarr, wdq, qan, wuq, qn, wdkv, kvn, cos, sin)


def _mla_kv_kernel(ckv_ref, kpe_ref, wuk_ref, wuv_ref, kn_ref, cos_ref, sin_ref, k_ref, v_ref):
    c = ckv_ref[...].astype(BF16)
    kn = _dot(c, wuk_ref[...])
    v_ref[...] = _dot(c, wuv_ref[...]).astype(BF16)
    kpe = kpe_ref[...]
    cos = cos_ref[...]
    sin = sin_ref[...]
    for j in range(kn.shape[1] // LANE):
        sl = slice(j * LANE, (j + 1) * LANE)
        kj = _rope(_head_norm128(kn[:, sl] + kpe, kn_ref[...], MLA_QK_DIM), cos, sin, MLA_ROPE // 4)
        k_ref[:, sl] = kj.astype(BF16)


def _mla_kv(ckv, kpe, wuk, wuv, kn, cos, sin, pos_blk):
    n, kv_lora = ckv.shape
    tm = TOK_TILE
    kw = wuk.shape[1]
    vw = wuv.shape[1]
    full = lambda shape: pl.BlockSpec(shape, lambda i: (0,) * len(shape))
    tok = lambda width: pl.BlockSpec((tm, width), lambda i: (i, 0))
    tab = pl.BlockSpec((tm, LANE), lambda i: (pos_blk(i), 0))
    return pl.pallas_call(
        _mla_kv_kernel,
        grid=(n // tm,),
        in_specs=[tok(kv_lora), tok(LANE), full((kv_lora, kw)), full((kv_lora, vw)), full((1, LANE)), tab, tab],
        out_specs=[tok(kw), tok(vw)],
        out_shape=[jax.ShapeDtypeStruct((n, kw), BF16), jax.ShapeDtypeStruct((n, vw), BF16)],
        compiler_params=_params(("arbitrary",)),
        name="mla_kv_expand",
    )(ckv, kpe, wuk, wuv, kn, cos, sin)


def _sort_pairs(n):
    pairs = []
    p = 1
    while p < n:
        k = p
        while k >= 1:
            for j in range(k % p, n - k, 2 * k):
                for i in range(min(k, n - j - k)):
                    if (i + j) // (2 * p) == (i + j + k) // (2 * p):
                        pairs.append((i + j, i + j + k))
            k //= 2
        p *= 2
    return pairs


def _cmpx(x, i, j):
    a, b = x[i], x[j]
    if b is None:
        return
    if a is None:
        x[i], x[j] = b, None
        return
    x[i], x[j] = jnp.maximum(a, b), jnp.minimum(a, b)


def _top_sorted(x):
    k = PEER_TOPK
    x = list(x)
    for i, j in _sort_pairs(k):
        _cmpx(x, i, j)
    shift = 4
    while shift >= 1:
        y = []
        for i in range(k):
            other = x[k - 1 - i]
            other = None if other is None else pltpu.roll(other, 8 - shift, 0)
            if x[i] is None:
                y.append(other)
            elif other is None:
                y.append(x[i])
            else:
                y.append(jnp.maximum(x[i], other))
        d = k // 2
        while d >= 1:
            for i in range(k):
                if (i & d) == 0:
                    _cmpx(y, i, i + d)
            d //= 2
        x = y
        shift //= 2
    return x


def _pair_candidates(a_ref, b_ref, scale):
    k = PEER_TOPK
    b_lo = b_ref[0:8, :]
    sub = lax.broadcasted_iota(jnp.int32, b_lo.shape, 0)
    row = lambda r: a_ref[r:r + 1, :] * scale
    out = [row(0) * b_lo, row(0) * b_ref[8:16, :]]
    for r in range(1, 8):
        n_r = k // (r + 1)
        c = row(r) * b_lo
        out.append(c if n_r >= 8 else jnp.where(sub < n_r, c, -1.0))
    out.append((a_ref[8:16, :] * scale) * b_ref[0:1, :])
    return out + [None] * (k - len(out))


def _peer_a_kernel(n_heads, x_ref, n_ref, sh_ref, sc_ref, wqt_ref, keys_ref,
                   ht_ref, e1_ref, e0n_ref, tau_ref, qt_sc, e_sc, top_sc):
    k = PEER_TOPK
    h = _modulate(x_ref[...], n_ref[...], sh_ref[0], sc_ref[0])
    ht = h.T.astype(BF16)
    ht_ref[...] = ht
    qt_sc[...] = _dot(wqt_ref[...], ht)

    def per_set(hc, carry):
        qb = qt_sc[pl.ds(pl.multiple_of(hc * PEER_HALF, PEER_HALF), PEER_HALF), :]
        s = _dot(keys_ref[hc], qb.astype(BF16))
        e = jnp.exp(s - jnp.max(s, axis=0, keepdims=True))
        e_sc[hc] = e
        top = _top_sorted([e[g * 8:(g + 1) * 8, :] for g in range(PEER_N_KEYS // 8)])
        for r in range(k):
            top_sc[hc, r:r + 1, :] = top[r][0:1, :]
        return carry

    lax.fori_loop(0, 2 * n_heads, per_set, 0)

    def per_head(hh, carry):
        a_ref = top_sc.at[2 * hh]
        b_ref = top_sc.at[2 * hh + 1]
        top = _top_sorted(_pair_candidates(a_ref, b_ref, 1.0))
        z = top[0][0:1, :]
        for r in range(1, k):
            z = z + top[r][0:1, :]
        rz = 1.0 / z
        e0n_ref[hh] = e_sc[2 * hh] * rz
        e1_ref[hh] = e_sc[2 * hh + 1]
        topn = _top_sorted(_pair_candidates(a_ref, b_ref, rz))
        tau_ref[pl.ds(hh, 1), :] = topn[k - 1][0:1, :]
        return carry

    lax.fori_loop(0, n_heads, per_head, 0)


def _peer_a(x, rows, norm, modarr, wqt, keys):
    n, d = x.shape
    t = PEER_A_TILE
    n_sets = keys.shape[0]
    n_heads = n_sets // 2
    full = lambda shape: pl.BlockSpec(shape, lambda i: (0,) * len(shape))
    return pl.pallas_call(
        functools.partial(_peer_a_kernel, n_heads),
        grid=(n // t,),
        in_specs=[pl.BlockSpec((t, d), lambda i: (i, 0)), full((1, d)),
                  _mod_spec(rows, t, 3, d), _mod_spec(rows, t, 4, d),
                  full(wqt.shape), full(keys.shape)],
        out_specs=[pl.BlockSpec((d, t), lambda i: (0, i)),
                   pl.BlockSpec((n_heads, PEER_N_KEYS, t), lambda i: (0, 0, i)),
                   pl.BlockSpec((n_heads, PEER_N_KEYS, t), lambda i: (0, 0, i)),
                   pl.BlockSpec((n_heads, t), lambda i: (0, i))],
        out_shape=[jax.ShapeDtypeStruct((d, n), BF16),
                   jax.ShapeDtypeStruct((n_heads, PEER_N_KEYS, n), F32),
                   jax.ShapeDtypeStruct((n_heads, PEER_N_KEYS, n), F32),
                   jax.ShapeDtypeStruct((n_heads, n), F32)],
        scratch_shapes=[pltpu.VMEM((n_sets * PEER_HALF, t), F32),
                        pltpu.VMEM((n_sets, PEER_N_KEYS, t), F32),
                        pltpu.VMEM((n_sets, PEER_TOPK, t), F32)],
        compiler_params=_params(("arbitrary",)),
        name="peer_select",
    )(x, norm, modarr, modarr, wqt, keys)


def _peer_b_kernel(n_heads, n_chunks, ht_ref, u_ref, vt_ref, e1_ref, e0n_ref, tau_ref, x_ref, g_ref, y_ref,
                   acc_sc, *chunk_sc):
    at_sc, w_sc = chunk_sc[:n_chunks], chunk_sc[n_chunks:]
    e = pl.program_id(1)

    @pl.when(e == 0)
    def _():
        acc_sc[...] = jnp.zeros_like(acc_sc)

    ht = ht_ref[...]
    t = ht.shape[1]
    ch = PEER_B_CHUNK
    for c in range(n_chunks):
        at_sc[c][...] = _dot(u_ref[c * ch:(c + 1) * ch, :], ht)
    for c in range(n_chunks):
        for r0 in range(0, ch, 64):
            ii, j0 = divmod(c * ch + r0, PEER_N_KEYS)
            for l0 in range(0, t, LANE):
                ls = slice(l0, l0 + LANE)
                gate = None
                for hh in range(n_heads):
                    p = e1_ref[hh, j0:j0 + 64, ls] * e0n_ref[hh, ii:ii + 1, ls]
                    g = jnp.where(p >= tau_ref[hh:hh + 1, ls], p, 0.0)
                    gate = g if gate is None else gate + g
                w_sc[c][r0:r0 + 64, ls] = (gate * jax.nn.gelu(at_sc[c][r0:r0 + 64, ls])).astype(BF16)
        acc_sc[...] += _dot(vt_ref[:, c * ch:(c + 1) * ch], w_sc[c][...])

    @pl.when(e == pl.num_programs(1) - 1)
    def _():
        y_ref[...] = x_ref[...] + g_ref[0] * acc_sc[...].T


def _peer_b(ht, u, vt, e1, e0n, tau, x, rows, modarr):
    n, d = x.shape
    t = PEER_B_TILE
    te = PEER_E_TILE
    n_exp = u.shape[0]
    n_heads = e1.shape[0]
    ti = te // PEER_N_KEYS
    n_chunks = te // PEER_B_CHUNK
    return pl.pallas_call(
        functools.partial(_peer_b_kernel, n_heads, n_chunks),
        grid=(n // t, n_exp // te),
        in_specs=[pl.BlockSpec((d, t), lambda i, e: (0, i)),
                  pl.BlockSpec((te, d), lambda i, e: (e, 0)),
                  pl.BlockSpec((d, te), lambda i, e: (0, e)),
                  pl.BlockSpec((n_heads, PEER_N_KEYS, t), lambda i, e: (0, 0, i)),
                  pl.BlockSpec((n_heads, ti, t), lambda i, e: (0, e, i)),
                  pl.BlockSpec((n_heads, t), lambda i, e: (0, i)),
                  pl.BlockSpec((t, d), lambda i, e: (i, 0)),
                  pl.BlockSpec((1, 1, d), lambda i, e: (rows.mod_row(i, t) * 6 + 5, 0, 0))],
        out_specs=pl.BlockSpec((t, d), lambda i, e: (i, 0)),
        out_shape=jax.ShapeDtypeStruct((n, d), F32),
        scratch_shapes=([pltpu.VMEM((d, t), F32)] + [pltpu.VMEM((PEER_B_CHUNK, t), F32)] * n_chunks
                        + [pltpu.VMEM((PEER_B_CHUNK, t), BF16)] * n_chunks),
        compiler_params=_params(("arbitrary", "arbitrary")),
        name="peer_experts",
    )(ht, u, vt, e1, e0n, tau, x, modarr)


def _pad_heads(w, n_heads, width):
    k = w.shape[0]
    w = w.reshape(k, n_heads, width)
    return jnp.pad(w, ((0, 0), (0, 0), (0, LANE - width))).reshape(k, n_heads * LANE)


def _pad_lanes(g, lo=0):
    g = g.reshape(1, -1).astype(F32)
    return jnp.pad(g, ((0, 0), (lo, LANE - lo - g.shape[1])))


def kernel(x_prompt, x_sample, cache_da_k, cache_da_v, cache_mla_ckv, cache_mla_kpe, c, c_ctx, norm1, norm2, ada_w, ada_b, da_wq, da_wk, da_wv, da_wo, da_q_norm, da_k_norm, da_lq1, da_lk1, da_lq2, da_lk2, da_subln, mla_wdq, mla_qa_norm, mla_wuq, mla_wdkv, mla_kv_norm, mla_wukv, mla_q_norm, mla_k_norm, mla_wo, peer_wq, peer_keys, peer_u, peer_v):
    bp, sp, d = x_prompt.shape
    bs, ss, _ = x_sample.shape
    depth = norm1.shape[0]
    past = cache_da_k.shape[2]
    n_p = bp * sp
    rows = _Rows(n_p, bs, ss)
    tm = TOK_TILE
    assert sp % tm == 0 and ss % tm == 0 and n_p % ss == 0 and past == sp
    assert n_p % PEER_B_TILE == 0 and ss % PEER_B_TILE == 0

    da_heads = da_wv.shape[2] // DA_V_DIM
    mla_heads = mla_wo.shape[1] // MLA_V_DIM
    kv_lora = mla_kv_norm.shape[1]
    peer_heads = peer_keys.shape[1]

    x = jnp.concatenate([x_prompt.reshape(n_p, d), x_sample.reshape(bs * ss, d)], axis=0)

    n_cond = 1 + bs
    r_pad = -(-n_cond // 8) * 8
    cvecs = jnp.concatenate([c_ctx[None, :], c, jnp.zeros((r_pad - n_cond, d), F32)], axis=0)
    mods = _modulation(cvecs, ada_w, ada_b).reshape(depth, r_pad * 6, 1, d)

    da_cos, da_sin = _rope_tables(ss, tm, 0, DA_HEAD_DIM, DA_HEAD_DIM)
    mla_cos, mla_sin = _rope_tables(ss, tm, MLA_NOPE, MLA_ROPE, LANE)
    ident = ss // tm

    new_da_k, new_da_v, new_ckv, new_kpe = [], [], [], []
    for l in range(depth):
        modarr = mods[l]
        j = l // 2
        if l % 2 == 0:
            lam_init = 0.8 - 0.6 * math.exp(-0.3 * l)
            q, kb, vb, kf, vf = _da_qkv(
                x, rows, norm1[l][None, :], modarr,
                da_wq[j].astype(BF16), da_wk[j].astype(BF16), da_wv[j].astype(BF16),
                jnp.tile(da_q_norm[j][None, :], (1, 2)), jnp.tile(da_k_norm[j][None, :], (1, 2)), da_cos, da_sin)
            new_da_k.append(kf[:n_p].reshape(bp, sp, 2 * da_heads, DA_HEAD_DIM))
            new_da_v.append(vf[:n_p].reshape(bp, sp, da_heads, DA_V_DIM))
            lam_vecs = jnp.stack([da_lq1[j], da_lk1[j], da_lq2[j], da_lk2[j]]).astype(F32)
            extra = [lam_vecs, da_subln[j][None, :]]
            extra_specs = [pl.BlockSpec(lam_vecs.shape, lambda b, hp, qi: (0, 0)),
                           pl.BlockSpec((1, DA_V_DIM), lambda b, hp, qi: (0, 0))]
            ck = cache_da_k[:, j].reshape(bs * past, 2 * da_heads * DA_HEAD_DIM)
            cv = cache_da_v[:, j].reshape(bs * past, da_heads * DA_V_DIM)
            o_p = _attention(functools.partial(_da_attn_kernel, lam_init, 1), da_heads, LANE, q, sp, 0, bp,
                             [(kb, vb, sp, 0)], extra, extra_specs, "da_attn_prompt")
            o_s = _attention(functools.partial(_da_attn_kernel, lam_init, 2), da_heads, LANE, q, ss, n_p, bs,
                             [(kb, vb, ss, n_p), (ck, cv, past, 0)], extra, extra_specs, "da_attn_latent")
            wo = da_wo[j].astype(BF16)
        else:
            wuq = _pad_heads(mla_wuq[j], mla_heads, MLA_QK_DIM).astype(BF16)
            wdkv = jnp.concatenate([
                mla_wdkv[j][:, :kv_lora],
                jnp.pad(mla_wdkv[j][:, kv_lora:], ((0, 0), (MLA_NOPE, LANE - MLA_QK_DIM)))], axis=1).astype(BF16)
            wukv = mla_wukv[j].reshape(kv_lora, mla_heads, MLA_NOPE + MLA_V_DIM)
            wuk = jnp.pad(wukv[:, :, :MLA_NOPE], ((0, 0), (0, 0), (0, LANE - MLA_NOPE)))
            wuk = wuk.reshape(kv_lora, mla_heads * LANE).astype(BF16)
            wuv = wukv[:, :, MLA_NOPE:].reshape(kv_lora, mla_heads * MLA_V_DIM).astype(BF16)
            q, ckv, kpe = _mla_q(
                x, rows, norm1[l][None, :], modarr, mla_wdq[j].astype(BF16), mla_qa_norm[j][None, :], wuq,
                _pad_lanes(mla_q_norm[j]), wdkv, mla_kv_norm[j][None, :], mla_cos, mla_sin)
            new_ckv.append(ckv[:n_p].reshape(bp, sp, kv_lora))
            new_kpe.append(kpe[:n_p, MLA_NOPE:MLA_QK_DIM].reshape(bp, sp, MLA_ROPE))
            kn = _pad_lanes(mla_k_norm[j])
            k, v = _mla_kv(ckv, kpe, wuk, wuv, kn, mla_cos, mla_sin, lambda i: rows.pos_blk(i, tm))
            c_ckv = cache_mla_ckv[:, j].reshape(bs * past, kv_lora)
            c_kpe = jnp.pad(cache_mla_kpe[:, j].reshape(bs * past, MLA_ROPE), ((0, 0), (MLA_NOPE, LANE - MLA_QK_DIM)))
            ck, cv = _mla_kv(c_ckv, c_kpe, wuk, wuv, kn, mla_cos, mla_sin, lambda i: ident)
            o_p = _attention(functools.partial(_mla_attn_kernel, 1), mla_heads // 2, 2 * LANE, q, sp, 0, bp,
                             [(k, v, sp, 0)], [], [], "mla_attn_prompt")
            o_s = _attention(functools.partial(_mla_attn_kernel, 2), mla_heads // 2, 2 * LANE, q, ss, n_p, bs,
                             [(k, v, ss, n_p), (ck, cv, past, 0)], [], [], "mla_attn_latent")
            wo = mla_wo[j].astype(BF16)
        x = _out_proj(o_p, o_s, wo, x, rows, modarr, 2)

        wqt = peer_wq[l].T.astype(BF16)
        keys = peer_keys[l].reshape(2 * peer_heads, PEER_N_KEYS, PEER_HALF).astype(BF16)
        ht, e1, e0n, tau = _peer_a(x, rows, norm2[l][None, :], modarr, wqt, keys)
        x = _peer_b(ht, peer_u[l].astype(BF16), peer_v[l].T.astype(BF16), e1, e0n, tau, x, rows, modarr)

    return (x[:n_p].reshape(bp, sp, d), x[n_p:].reshape(bs, ss, d),
            jnp.stack(new_da_k, axis=1), jnp.stack(new_da_v, axis=1),
            jnp.stack(new_ckv, axis=1), jnp.stack(new_kpe, axis=1))
```

```python
import functools
import math

import jax
import jax.numpy as jnp
from jax import lax
from jax.experimental import pallas as pl
from jax.experimental.pallas import tpu as pltpu

GRID_W = 64
DA_HEAD_DIM = 64
DA_V_DIM = 2 * DA_HEAD_DIM
MLA_NOPE = 64
MLA_ROPE = 32
MLA_QK_DIM = MLA_NOPE + MLA_ROPE
MLA_V_DIM = 64
PEER_N_KEYS = 128
PEER_HALF = 128
PEER_TOPK = 16
ROPE_BASE = 10000.0
EPS = 1e-6

LANE = 128
VMEM_LIMIT = 56 * 1024 * 1024
TOK_TILE = 256
Q_TILE = 1024
KEY_CHUNK = 1024
LOG2E = math.log2(math.e)
PEER_A_TILE = 256
PEER_B_TILE = 512
PEER_E_TILE = 2048
PEER_B_CHUNK = 256

BF16 = jnp.bfloat16
F32 = jnp.float32


def _params(sem, flags=None):
    return pltpu.CompilerParams(dimension_semantics=sem, vmem_limit_bytes=VMEM_LIMIT, flags=flags)


def _dot(a, b):
    return jnp.dot(a, b, preferred_element_type=F32)


def _dot_nt(a, b):
    return lax.dot_general(a, b, (((1,), (1,)), ((), ())), preferred_element_type=F32)


def _rms(x):
    return x * lax.rsqrt(jnp.mean(x * x, axis=-1, keepdims=True) + EPS)


def _modulate(x, g, shift, scale):
    return _rms(x) * g * (1.0 + scale) + shift


def _rope(x, cos, sin, half):
    lane = lax.broadcasted_iota(jnp.int32, x.shape, 1)
    first = (lane & (2 * half - 1)) < half
    partner = jnp.where(first, pltpu.roll(x, LANE - half, 1), pltpu.roll(x, half, 1))
    return x * cos + partner * sin


class _Rows:
    def __init__(self, n_p, n_b, s_lat):
        self.n_p, self.n_b, self.s_lat = n_p, n_b, s_lat
        self.n = n_p + n_b * s_lat

    def mod_row(self, i, tm):
        np_t = self.n_p // tm
        return jnp.where(i < np_t, 0, 1 + (i - np_t) // (self.s_lat // tm))

    def pos_blk(self, i, tm):
        np_t = self.n_p // tm
        return jnp.where(i < np_t, self.s_lat // tm, (i - np_t) % (self.s_lat // tm))


def _mod_spec(rows, tm, which, d):
    return pl.BlockSpec((1, 1, d), lambda i: (rows.mod_row(i, tm) * 6 + which, 0, 0))


def _rope_tables(s_lat, tm, rot_lo, rot_w, period):
    hd = rot_w // 2
    half = hd // 2
    lane = jnp.arange(LANE)
    d = (lane % period) - rot_lo
    rot = (d >= 0) & (d < rot_w)
    d = jnp.clip(d, 0, rot_w - 1)
    use_row = d < hd
    dd = d % hd
    i = dd % half
    first = dd < half
    inv = 1.0 / (ROPE_BASE ** ((2 * i).astype(F32) / hd))
    t = jnp.arange(s_lat)
    row = (t // GRID_W).astype(F32)
    col = (t % GRID_W).astype(F32)
    pos = jnp.where(use_row[None, :], row[:, None], col[:, None])
    ang = pos * inv[None, :]
    cos = jnp.where(rot[None, :], jnp.cos(ang), 1.0)
    sin = jnp.where(rot[None, :], jnp.where(first[None, :], -jnp.sin(ang), jnp.sin(ang)), 0.0)
    cos = jnp.concatenate([cos, jnp.ones((tm, LANE), F32)], axis=0)
    sin = jnp.concatenate([sin, jnp.zeros((tm, LANE), F32)], axis=0)
    return cos.astype(F32), sin.astype(F32)


def _mod_kernel(c_ref, w_ref, b_ref, o_ref):
    c = c_ref[...]
    s = c * jax.nn.sigmoid(c)
    o_ref[...] = _dot(s.astype(BF16), w_ref[...].astype(BF16)) + b_ref[...]


def _modulation(cvecs, ada_w, ada_b):
    depth, d, d6 = ada_w.shape
    r = cvecs.shape[0]
    tn = d
    return pl.pallas_call(
        _mod_kernel,
        grid=(depth, d6 // tn),
        in_specs=[
            pl.BlockSpec((r, d), lambda l, j: (0, 0)),
            pl.BlockSpec((None, d, tn), lambda l, j: (l, 0, j)),
            pl.BlockSpec((None, 1, tn), lambda l, j: (l, 0, j)),
        ],
        out_specs=pl.BlockSpec((None, r, tn), lambda l, j: (l, 0, j)),
        out_shape=jax.ShapeDtypeStruct((depth, r, d6), F32),
        compiler_params=_params(("arbitrary", "arbitrary")),
        name="adaln_modulation",
    )(cvecs, ada_w, ada_b.reshape(depth, 1, d6))


def _da_qkv_kernel(x_ref, n_ref, sh_ref, sc_ref, wq_ref, wk_ref, wv_ref, qn_ref, kn_ref, cos_ref, sin_ref,
                   q_ref, kb_ref, vb_ref, kf_ref, vf_ref):
    h = _modulate(x_ref[...], n_ref[...], sh_ref[0], sc_ref[0]).astype(BF16)
    q = _dot(h, wq_ref[...])
    k = _dot(h, wk_ref[...])
    v = _dot(h, wv_ref[...])
    vf_ref[...] = v
    vb_ref[...] = v.astype(BF16)
    cos = cos_ref[...]
    sin = sin_ref[...]
    lane = lax.broadcasted_iota(jnp.int32, (x_ref.shape[0], LANE), 1)
    lo = lane < DA_HEAD_DIM

    def head_norm(t, g):
        sq = t * t
        s_lo = jnp.sum(jnp.where(lo, sq, 0.0), axis=-1, keepdims=True)
        s_hi = jnp.sum(jnp.where(lo, 0.0, sq), axis=-1, keepdims=True)
        ms = jnp.where(lo, s_lo, s_hi) * (1.0 / DA_HEAD_DIM)
        return t * lax.rsqrt(ms + EPS) * g

    scale = DA_HEAD_DIM ** -0.5 * LOG2E
    for j in range(q.shape[1] // LANE):
        sl = slice(j * LANE, (j + 1) * LANE)
        qj = _rope(head_norm(q[:, sl], qn_ref[...]), cos, sin, DA_HEAD_DIM // 4)
        kj = _rope(head_norm(k[:, sl], kn_ref[...]), cos, sin, DA_HEAD_DIM // 4)
        q_ref[:, sl] = (qj * scale).astype(BF16)
        kf_ref[:, sl] = kj
        kb_ref[:, sl] = kj.astype(BF16)


def _da_qkv(x, rows, norm, modarr, wq, wk, wv, qn, kn, cos, sin):
    n, d = x.shape
    tm = TOK_TILE
    w = wq.shape[1]
    full = lambda shape: pl.BlockSpec(shape, lambda i: (0,) * len(shape))
    tok = lambda width: pl.BlockSpec((tm, width), lambda i: (i, 0))
    tab = pl.BlockSpec((tm, LANE), lambda i: (rows.pos_blk(i, tm), 0))
    return pl.pallas_call(
        _da_qkv_kernel,
        grid=(n // tm,),
        in_specs=[tok(d), full((1, d)), _mod_spec(rows, tm, 0, d), _mod_spec(rows, tm, 1, d),
                  full((d, w)), full((d, w)), full((d, w)), full((1, LANE)), full((1, LANE)), tab, tab],
        out_specs=[tok(w)] * 5,
        out_shape=[jax.ShapeDtypeStruct((n, w), BF16)] * 3 + [jax.ShapeDtypeStruct((n, w), F32)] * 2,
        compiler_params=_params(("arbitrary",)),
        name="da_qkv",
    )(x, norm, modarr, modarr, wq, wk, wv, qn, kn, cos, sin)


def _attend(qs, ksls, vsl, kv_refs):
    n = len(qs)
    m, acc = [None] * n, [None] * n
    for k_ref, v_ref in kv_refs:
        sk = k_ref.shape[0]
        step = min(KEY_CHUNK, sk)
        for c0 in range(0, sk, step):
            v = v_ref[c0:c0 + step, vsl].astype(BF16)
            v1 = jnp.concatenate([v, jnp.ones_like(v)], axis=1)
            for a in range(n):
                s = _dot_nt(qs[a], k_ref[c0:c0 + step, ksls[a]].astype(BF16))
                mx = s.max(axis=-1, keepdims=True)
                if m[a] is None:
                    m[a] = mx
                    acc[a] = _dot(jnp.exp2((s - mx).astype(BF16)), v1)
                else:
                    m_new = jnp.maximum(m[a], mx)
                    alpha = jnp.exp2(m[a] - m_new)
                    acc[a] = alpha * acc[a] + _dot(jnp.exp2((s - m_new).astype(BF16)), v1)
                    m[a] = m_new
    return [acc[a][:, :LANE] / acc[a][:, LANE:] for a in range(n)]


def _da_attn_kernel(lam_init, n_chunks, group, lam_ref, subln_ref, q_ref, *refs):
    kv = [(refs[2 * c], refs[2 * c + 1]) for c in range(n_chunks)]
    o_ref = refs[2 * n_chunks]
    lv = lam_ref[...]
    lam = (jnp.exp(jnp.sum(lv[0:1] * lv[1:2], axis=-1, keepdims=True))
           - jnp.exp(jnp.sum(lv[2:3] * lv[3:4], axis=-1, keepdims=True)) + lam_init)
    for g in range(group):
        sl = slice(g * LANE, (g + 1) * LANE)
        q = q_ref[:, sl]
        lane = lax.broadcasted_iota(jnp.int32, q.shape, 1)
        zero = jnp.zeros_like(q)
        qs = [jnp.where(lane < DA_HEAD_DIM, q, zero), jnp.where(lane < DA_HEAD_DIM, zero, q)]
        o1, o2 = _attend(qs, [sl] * 2, sl, kv)
        o = o1 - lam * o2
        o = _rms(o) * subln_ref[...] * (1.0 - lam_init)
        o_ref[:, sl] = o.astype(o_ref.dtype)


def _mla_attn_kernel(n_chunks, group, q_ref, *refs):
    kv = [(refs[2 * c], refs[2 * c + 1]) for c in range(n_chunks)]
    o_ref = refs[2 * n_chunks]
    for g in range(group):
        sls = [slice((2 * g + half) * LANE, (2 * g + half + 1) * LANE) for half in range(2)]
        vsl = slice(g * LANE, (g + 1) * LANE)
        outs = _attend([q_ref[:, sl] for sl in sls], sls, vsl, kv)
        lane = lax.broadcasted_iota(jnp.int32, outs[0].shape, 1)
        o_ref[:, vsl] = jnp.where(lane < MLA_V_DIM, outs[0], outs[1]).astype(o_ref.dtype)


def _attention(kernel, n_pairs, group, qw, q, seq, row_off, n_b, chunks, extra_in, extra_specs, name):
    tq = min(Q_TILE, seq)
    nq = seq // tq
    q_spec = pl.BlockSpec((tq, qw * group), lambda b, hp, qi: (row_off // tq + b * nq + qi, hp))
    in_specs = list(extra_specs) + [q_spec]
    args = list(extra_in) + [q]
    for k, v, sk, off in chunks:
        kv_map = functools.partial(lambda b, hp, qi, o, s: (o // s + b, hp), o=off, s=sk)
        in_specs.append(pl.BlockSpec((sk, qw * group), kv_map))
        in_specs.append(pl.BlockSpec((sk, LANE * group), kv_map))
        args += [k, v]
    return pl.pallas_call(
        functools.partial(kernel, group),
        grid=(n_b, n_pairs // group, nq),
        in_specs=in_specs,
        out_specs=pl.BlockSpec((tq, LANE * group), lambda b, hp, qi: (b * nq + qi, hp)),
        out_shape=jax.ShapeDtypeStruct((n_b * seq, n_pairs * LANE), BF16),
        compiler_params=_params(("arbitrary", "arbitrary", "arbitrary")),
        name=name,
    )(*args)


def _out_proj_kernel(n_prompt_tiles, op_ref, os_ref, w_ref, x_ref, g_ref, y_ref):
    o = jnp.where(pl.program_id(0) < n_prompt_tiles, op_ref[...], os_ref[...])
    y_ref[...] = x_ref[...] + g_ref[0] * _dot(o, w_ref[...])


def _out_proj(o_p, o_s, w, x, rows, modarr, which):
    n, d = x.shape
    tm = TOK_TILE
    k = o_p.shape[1]
    np_t = rows.n_p // tm
    return pl.pallas_call(
        functools.partial(_out_proj_kernel, np_t),
        grid=(n // tm,),
        in_specs=[pl.BlockSpec((tm, k), lambda i: (jnp.minimum(i, np_t - 1), 0)),
                  pl.BlockSpec((tm, k), lambda i: (jnp.maximum(i - np_t, 0), 0)),
                  pl.BlockSpec((k, d), lambda i: (0, 0)),
                  pl.BlockSpec((tm, d), lambda i: (i, 0)), _mod_spec(rows, tm, which, d)],
        out_specs=pl.BlockSpec((tm, d), lambda i: (i, 0)),
        out_shape=jax.ShapeDtypeStruct((n, d), F32),
        compiler_params=_params(("arbitrary",)),
        name="out_proj",
    )(o_p, o_s, w, x, modarr)


def _head_norm128(t, g, width):
    ms = jnp.sum(t * t, axis=-1, keepdims=True) * (1.0 / width)
    return t * lax.rsqrt(ms + EPS) * g


def _mla_q_kernel(kv_lora, x_ref, n_ref, sh_ref, sc_ref, wdq_ref, qan_ref, wuq_ref, qn_ref, wdkv_ref, kvn_ref,
                  cos_ref, sin_ref, q_ref, ckv_ref, kpe_ref):
    h = _modulate(x_ref[...], n_ref[...], sh_ref[0], sc_ref[0]).astype(BF16)
    cq = (_rms(_dot(h, wdq_ref[...])) * qan_ref[...]).astype(BF16)
    q = _dot(cq, wuq_ref[...])
    cos = cos_ref[...]
    sin = sin_ref[...]
    scale = MLA_QK_DIM ** -0.5 * LOG2E
    for j in range(q.shape[1] // LANE):
        sl = slice(j * LANE, (j + 1) * LANE)
        qj = _rope(_head_norm128(q[:, sl], qn_ref[...], MLA_QK_DIM), cos, sin, MLA_ROPE // 4)
        q_ref[:, sl] = (qj * scale).astype(BF16)
    ckvf = _dot(h, wdkv_ref[...])
    ckv_ref[...] = _rms(ckvf[:, :kv_lora]) * kvn_ref[...]
    kpe_ref[...] = ckvf[:, kv_lora:]


def _mla_q(x, rows, norm, modarr, wdq, qan, wuq, qn, wdkv, kvn, cos, sin):
    n, d = x.shape
    tm = TOK_TILE
    ql = wdq.shape[1]
    qw = wuq.shape[1]
    kv_lora = kvn.shape[1]
    full = lambda shape: pl.BlockSpec(shape, lambda i: (0,) * len(shape))
    tok = lambda width: pl.BlockSpec((tm, width), lambda i: (i, 0))
    tab = pl.BlockSpec((tm, LANE), lambda i: (rows.pos_blk(i, tm), 0))
    return pl.pallas_call(
        functools.partial(_mla_q_kernel, kv_lora),
        grid=(n // tm,),
        in_specs=[tok(d), full((1, d)), _mod_spec(rows, tm, 0, d), _mod_spec(rows, tm, 1, d),
                  full((d, ql)), full((1, ql)), full((ql, qw)), full((1, LANE)),
                  full((d, kv_lora + LANE)), full((1, kv_lora)), tab, tab],
        out_specs=[tok(qw), tok(kv_lora), tok(LANE)],
        out_shape=[jax.ShapeDtypeStruct((n, qw), BF16), jax.ShapeDtypeStruct((n, kv_lora), F32),
                   jax.ShapeDtypeStruct((n, LANE), F32)],
        compiler_params=_params(("arbitrary",)),
        name="mla_q_ckv",
    )(x, norm, modarr, modarr, wdq, qan, wuq, qn, wdkv, kvn, cos, sin)


def _mla_kv_kernel(ckv_ref, kpe_ref, wuk_ref, wuv_ref, kn_ref, cos_ref, sin_ref, k_ref, v_ref):
    c = ckv_ref[...].astype(BF16)
    kn = _dot(c, wuk_ref[...])
    v_ref[...] = _dot(c, wuv_ref[...]).astype(BF16)
    kpe = kpe_ref[...]
    cos = cos_ref[...]
    sin = sin_ref[...]
    for j in range(kn.shape[1] // LANE):
        sl = slice(j * LANE, (j + 1) * LANE)
        kj = _rope(_head_norm128(kn[:, sl] + kpe, kn_ref[...], MLA_QK_DIM), cos, sin, MLA_ROPE // 4)
        k_ref[:, sl] = kj.astype(BF16)


def _mla_kv(ckv, kpe, wuk, wuv, kn, cos, sin, pos_blk):
    n, kv_lora = ckv.shape
    tm = TOK_TILE
    kw = wuk.shape[1]
    vw = wuv.shape[1]
    full = lambda shape: pl.BlockSpec(shape, lambda i: (0,) * len(shape))
    tok = lambda width: pl.BlockSpec((tm, width), lambda i: (i, 0))
    tab = pl.BlockSpec((tm, LANE), lambda i: (pos_blk(i), 0))
    return pl.pallas_call(
        _mla_kv_kernel,
        grid=(n // tm,),
        in_specs=[tok(kv_lora), tok(LANE), full((kv_lora, kw)), full((kv_lora, vw)), full((1, LANE)), tab, tab],
        out_specs=[tok(kw), tok(vw)],
        out_shape=[jax.ShapeDtypeStruct((n, kw), BF16), jax.ShapeDtypeStruct((n, vw), BF16)],
        compiler_params=_params(("arbitrary",)),
        name="mla_kv_expand",
    )(ckv, kpe, wuk, wuv, kn, cos, sin)


def _sort_pairs(n):
    pairs = []
    p = 1
    while p < n:
        k = p
        while k >= 1:
            for j in range(k % p, n - k, 2 * k):
                for i in range(min(k, n - j - k)):
                    if (i + j) // (2 * p) == (i + j + k) // (2 * p):
                        pairs.append((i + j, i + j + k))
            k //= 2
        p *= 2
    return pairs


def _cmpx(x, i, j):
    a, b = x[i], x[j]
    if b is None:
        return
    if a is None:
        x[i], x[j] = b, None
        return
    x[i], x[j] = jnp.maximum(a, b), jnp.minimum(a, b)


def _top_sorted(x):
    k = PEER_TOPK
    x = list(x)
    for i, j in _sort_pairs(k):
        _cmpx(x, i, j)
    shift = 4
    while shift >= 1:
        y = []
        for i in range(k):
            other = x[k - 1 - i]
            other = None if other is None else pltpu.roll(other, 8 - shift, 0)
            if x[i] is None:
                y.append(other)
            elif other is None:
                y.append(x[i])
            else:
                y.append(jnp.maximum(x[i], other))
        d = k // 2
        while d >= 1:
            for i in range(k):
                if (i & d) == 0:
                    _cmpx(y, i, i + d)
            d //= 2
        x = y
        shift //= 2
    return x


def _pair_candidates(a_ref, b_ref, scale):
    k = PEER_TOPK
    b_lo = b_ref[0:8, :]
    sub = lax.broadcasted_iota(jnp.int32, b_lo.shape, 0)
    row = lambda r: a_ref[r:r + 1, :] * scale
    out = [row(0) * b_lo, row(0) * b_ref[8:16, :]]
    for r in range(1, 8):
        n_r = k // (r + 1)
        c = row(r) * b_lo
        out.append(c if n_r >= 8 else jnp.where(sub < n_r, c, -1.0))
    out.append((a_ref[8:16, :] * scale) * b_ref[0:1, :])
    return out + [None] * (k - len(out))


def _peer_a_kernel(n_heads, x_ref, n_ref, sh_ref, sc_ref, wqt_ref, keys_ref,
                   ht_ref, e1_ref, e0n_ref, tau_ref, qt_sc, e_sc, top_sc):
    k = PEER_TOPK
    h = _modulate(x_ref[...], n_ref[...], sh_ref[0], sc_ref[0])
    ht = h.T.astype(BF16)
    ht_ref[...] = ht
    qt_sc[...] = _dot(wqt_ref[...], ht)

    for hc in range(2 * n_heads):
        qb = qt_sc[hc * PEER_HALF:(hc + 1) * PEER_HALF, :]
        s = _dot(keys_ref[hc], qb.astype(BF16))
        e_sc[hc] = jnp.exp(s - jnp.max(s, axis=0, keepdims=True))

    def per_set(hc, carry):
        e = e_sc[hc]
        top = _top_sorted([e[g * 8:(g + 1) * 8, :] for g in range(PEER_N_KEYS // 8)])
        for r in range(k):
            top_sc[hc, r:r + 1, :] = top[r][0:1, :]
        return carry

    lax.fori_loop(0, 2 * n_heads, per_set, 0)

    def per_head(hh, carry):
        a_ref = top_sc.at[2 * hh]
        b_ref = top_sc.at[2 * hh + 1]
        top = _top_sorted(_pair_candidates(a_ref, b_ref, 1.0))
        z = top[0][0:1, :]
        for r in range(1, k):
            z = z + top[r][0:1, :]
        rz = 1.0 / z
        e0n_ref[hh] = e_sc[2 * hh] * rz
        e1_ref[hh] = e_sc[2 * hh + 1]
        topn = _top_sorted(_pair_candidates(a_ref, b_ref, rz))
        tau_ref[pl.ds(hh, 1), :] = topn[k - 1][0:1, :]
        return carry

    lax.fori_loop(0, n_heads, per_head, 0)


def _peer_a(x, rows, norm, modarr, wqt, keys):
    n, d = x.shape
    t = PEER_A_TILE
    n_sets = keys.shape[0]
    n_heads = n_sets // 2
    full = lambda shape: pl.BlockSpec(shape, lambda i: (0,) * len(shape))
    return pl.pallas_call(
        functools.partial(_peer_a_kernel, n_heads),
        grid=(n // t,),
        in_specs=[pl.BlockSpec((t, d), lambda i: (i, 0)), full((1, d)),
                  _mod_spec(rows, t, 3, d), _mod_spec(rows, t, 4, d),
                  full(wqt.shape), full(keys.shape)],
        out_specs=[pl.BlockSpec((d, t), lambda i: (0, i)),
                   pl.BlockSpec((n_heads, PEER_N_KEYS, t), lambda i: (0, 0, i)),
                   pl.BlockSpec((n_heads, PEER_N_KEYS, t), lambda i: (0, 0, i)),
                   pl.BlockSpec((n_heads, t), lambda i: (0, i))],
        out_shape=[jax.ShapeDtypeStruct((d, n), BF16),
                   jax.ShapeDtypeStruct((n_heads, PEER_N_KEYS, n), F32),
                   jax.ShapeDtypeStruct((n_heads, PEER_N_KEYS, n), F32),
                   jax.ShapeDtypeStruct((n_heads, n), F32)],
        scratch_shapes=[pltpu.VMEM((n_sets * PEER_HALF, t), F32),
                        pltpu.VMEM((n_sets, PEER_N_KEYS, t), F32),
                        pltpu.VMEM((n_sets, PEER_TOPK, t), F32)],
        compiler_params=_params(("arbitrary",)),
        name="peer_select",
    )(x, norm, modarr, modarr, wqt, keys)


def _peer_b_kernel(n_heads, n_chunks, ht_ref, u_ref, vt_ref, e1_ref, e0n_ref, tau_ref, x_ref, g_ref, y_ref,
                   acc_sc, *chunk_sc):
    at_sc, w_sc = chunk_sc[:n_chunks], chunk_sc[n_chunks:]
    e = pl.program_id(1)

    @pl.when(e == 0)
    def _():
        acc_sc[...] = jnp.zeros_like(acc_sc)

    ht = ht_ref[...]
    t = ht.shape[1]
    ch = PEER_B_CHUNK
    for c in range(n_chunks):
        at_sc[c][...] = _dot(u_ref[c * ch:(c + 1) * ch, :], ht)
    for c in range(n_chunks):
        for r0 in range(0, ch, 64):
            ii, j0 = divmod(c * ch + r0, PEER_N_KEYS)
            for l0 in range(0, t, LANE):
                ls = slice(l0, l0 + LANE)
                gate = None
                for hh in range(n_heads):
                    p = e1_ref[hh, j0:j0 + 64, ls] * e0n_ref[hh, ii:ii + 1, ls]
                    g = jnp.where(p >= tau_ref[hh:hh + 1, ls], p, 0.0)
                    gate = g if gate is None else gate + g
                w_sc[c][r0:r0 + 64, ls] = (gate * jax.nn.gelu(at_sc[c][r0:r0 + 64, ls])).astype(BF16)
        acc_sc[...] += _dot(vt_ref[:, c * ch:(c + 1) * ch], w_sc[c][...])

    @pl.when(e == pl.num_programs(1) - 1)
    def _():
        y_ref[...] = x_ref[...] + g_ref[0] * acc_sc[...].T


def _peer_b(ht, u, vt, e1, e0n, tau, x, rows, modarr):
    n, d = x.shape
    t = PEER_B_TILE
    te = PEER_E_TILE
    n_exp = u.shape[0]
    n_heads = e1.shape[0]
    ti = te // PEER_N_KEYS
    n_chunks = te // PEER_B_CHUNK
    return pl.pallas_call(
        functools.partial(_peer_b_kernel, n_heads, n_chunks),
        grid=(n // t, n_exp // te),
        in_specs=[pl.BlockSpec((d, t), lambda i, e: (0, i)),
                  pl.BlockSpec((te, d), lambda i, e: (e, 0)),
                  pl.BlockSpec((d, te), lambda i, e: (0, e)),
                  pl.BlockSpec((n_heads, PEER_N_KEYS, t), lambda i, e: (0, 0, i)),
                  pl.BlockSpec((n_heads, ti, t), lambda i, e: (0, e, i)),
                  pl.BlockSpec((n_heads, t), lambda i, e: (0, i)),
                  pl.BlockSpec((t, d), lambda i, e: (i, 0)),
                  pl.BlockSpec((1, 1, d), lambda i, e: (rows.mod_row(i, t) * 6 + 5, 0, 0))],
        out_specs=pl.BlockSpec((t, d), lambda i, e: (i, 0)),
        out_shape=jax.ShapeDtypeStruct((n, d), F32),
        scratch_shapes=([pltpu.VMEM((d, t), F32)] + [pltpu.VMEM((PEER_B_CHUNK, t), F32)] * n_chunks
                        + [pltpu.VMEM((PEER_B_CHUNK, t), BF16)] * n_chunks),
        compiler_params=_params(("arbitrary", "arbitrary")),
        name="peer_experts",
    )(ht, u, vt, e1, e0n, tau, x, modarr)


def _pad_heads(w, n_heads, width):
    k = w.shape[0]
    w = w.reshape(k, n_heads, width)
    return jnp.pad(w, ((0, 0), (0, 0), (0, LANE - width))).reshape(k, n_heads * LANE)


def _pad_lanes(g, lo=0):
    g = g.reshape(1, -1).astype(F32)
    return jnp.pad(g, ((0, 0), (lo, LANE - lo - g.shape[1])))


def kernel(x_prompt, x_sample, cache_da_k, cache_da_v, cache_mla_ckv, cache_mla_kpe, c, c_ctx, norm1, norm2, ada_w, ada_b, da_wq, da_wk, da_wv, da_wo, da_q_norm, da_k_norm, da_lq1, da_lk1, da_lq2, da_lk2, da_subln, mla_wdq, mla_qa_norm, mla_wuq, mla_wdkv, mla_kv_norm, mla_wukv, mla_q_norm, mla_k_norm, mla_wo, peer_wq, peer_keys, peer_u, peer_v):
    bp, sp, d = x_prompt.shape
    bs, ss, _ = x_sample.shape
    depth = norm1.shape[0]
    past = cache_da_k.shape[2]
    n_p = bp * sp
    rows = _Rows(n_p, bs, ss)
    tm = TOK_TILE
    assert sp % tm == 0 and ss % tm == 0 and n_p % ss == 0 and past == sp
    assert n_p % PEER_B_TILE == 0 and ss % PEER_B_TILE == 0

    da_heads = da_wv.shape[2] // DA_V_DIM
    mla_heads = mla_wo.shape[1] // MLA_V_DIM
    kv_lora = mla_kv_norm.shape[1]
    peer_heads = peer_keys.shape[1]

    x = jnp.concatenate([x_prompt.reshape(n_p, d), x_sample.reshape(bs * ss, d)], axis=0)

    n_cond = 1 + bs
    r_pad = -(-n_cond // 8) * 8
    cvecs = jnp.concatenate([c_ctx[None, :], c, jnp.zeros((r_pad - n_cond, d), F32)], axis=0)
    mods = _modulation(cvecs, ada_w, ada_b).reshape(depth, r_pad * 6, 1, d)

    da_cos, da_sin = _rope_tables(ss, tm, 0, DA_HEAD_DIM, DA_HEAD_DIM)
    mla_cos, mla_sin = _rope_tables(ss, tm, MLA_NOPE, MLA_ROPE, LANE)
    ident = ss // tm

    new_da_k, new_da_v, new_ckv, new_kpe = [], [], [], []
    for l in range(depth):
        modarr = mods[l]
        j = l // 2
        if l % 2 == 0:
            lam_init = 0.8 - 0.6 * math.exp(-0.3 * l)
            q, kb, vb, kf, vf = _da_qkv(
                x, rows, norm1[l][None, :], modarr,
                da_wq[j].astype(BF16), da_wk[j].astype(BF16), da_wv[j].astype(BF16),
                jnp.tile(da_q_norm[j][None, :], (1, 2)), jnp.tile(da_k_norm[j][None, :], (1, 2)), da_cos, da_sin)
            new_da_k.append(kf[:n_p].reshape(bp, sp, 2 * da_heads, DA_HEAD_DIM))
            new_da_v.append(vf[:n_p].reshape(bp, sp, da_heads, DA_V_DIM))
            lam_vecs = jnp.stack([da_lq1[j], da_lk1[j], da_lq2[j], da_lk2[j]]).astype(F32)
            extra = [lam_vecs, da_subln[j][None, :]]
            extra_specs = [pl.BlockSpec(lam_vecs.shape, lambda b, hp, qi: (0, 0)),
                           pl.BlockSpec((1, DA_V_DIM), lambda b, hp, qi: (0, 0))]
            ck = cache_da_k[:, j].reshape(bs * past, 2 * da_heads * DA_HEAD_DIM)
            cv = cache_da_v[:, j].reshape(bs * past, da_heads * DA_V_DIM)
            o_p = _attention(functools.partial(_da_attn_kernel, lam_init, 1), da_heads, da_heads, LANE, q, sp, 0, bp,
                             [(kb, vb, sp, 0)], extra, extra_specs, "da_attn_prompt")
            o_s = _attention(functools.partial(_da_attn_kernel, lam_init, 2), da_heads, 1, LANE, q, ss, n_p, bs,
                             [(kb, vb, ss, n_p), (ck, cv, past, 0)], extra, extra_specs, "da_attn_latent")
            wo = da_wo[j].astype(BF16)
        else:
            wuq = _pad_heads(mla_wuq[j], mla_heads, MLA_QK_DIM).astype(BF16)
            wdkv = jnp.concatenate([
                mla_wdkv[j][:, :kv_lora],
                jnp.pad(mla_wdkv[j][:, kv_lora:], ((0, 0), (MLA_NOPE, LANE - MLA_QK_DIM)))], axis=1).astype(BF16)
            wukv = mla_wukv[j].reshape(kv_lora, mla_heads, MLA_NOPE + MLA_V_DIM)
            wuk = jnp.pad(wukv[:, :, :MLA_NOPE], ((0, 0), (0, 0), (0, LANE - MLA_NOPE)))
            wuk = wuk.reshape(kv_lora, mla_heads * LANE).astype(BF16)
            wuv = wukv[:, :, MLA_NOPE:].reshape(kv_lora, mla_heads * MLA_V_DIM).astype(BF16)
            q, ckv, kpe = _mla_q(
                x, rows, norm1[l][None, :], modarr, mla_wdq[j].astype(BF16), mla_qa_norm[j][None, :], wuq,
                _pad_lanes(mla_q_norm[j]), wdkv, mla_kv_norm[j][None, :], mla_cos, mla_sin)
            new_ckv.append(ckv[:n_p].reshape(bp, sp, kv_lora))
            new_kpe.append(kpe[:n_p, MLA_NOPE:MLA_QK_DIM].reshape(bp, sp, MLA_ROPE))
            kn = _pad_lanes(mla_k_norm[j])
            k, v = _mla_kv(ckv, kpe, wuk, wuv, kn, mla_cos, mla_sin, lambda i: rows.pos_blk(i, tm))
            c_ckv = cache_mla_ckv[:, j].reshape(bs * past, kv_lora)
            c_kpe = jnp.pad(cache_mla_kpe[:, j].reshape(bs * past, MLA_ROPE), ((0, 0), (MLA_NOPE, LANE - MLA_QK_DIM)))
            ck, cv = _mla_kv(c_ckv, c_kpe, wuk, wuv, kn, mla_cos, mla_sin, lambda i: ident)
            o_p = _attention(functools.partial(_mla_attn_kernel, 1), mla_heads // 2, mla_heads // 2, 2 * LANE, q, sp, 0, bp,
                             [(k, v, sp, 0)], [], [], "mla_attn_prompt")
            o_s = _attention(functools.partial(_mla_attn_kernel, 2), mla_heads // 2, 1, 2 * LANE, q, ss, n_p, bs,
                             [(k, v, ss, n_p), (ck, cv, past, 0)], [], [], "mla_attn_latent")
            wo = mla_wo[j].astype(BF16)
        x = _out_proj(o_p, o_s, wo, x, rows, modarr, 2)

        wqt = peer_wq[l].T.astype(BF16)
        keys = peer_keys[l].reshape(2 * peer_heads, PEER_N_KEYS, PEER_HALF).astype(BF16)
        ht, e1, e0n, tau = _peer_a(x, rows, norm2[l][None, :], modarr, wqt, keys)
        x = _peer_b(ht, peer_u[l].astype(BF16), peer_v[l].T.astype(BF16), e1, e0n, tau, x, rows, modarr)

    return (x[:n_p].reshape(bp, sp, d), x[n_p:].reshape(bs, ss, d),
            jnp.stack(new_da_k, axis=1), jnp.stack(new_da_v, axis=1),
            jnp.stack(new_ckv, axis=1), jnp.stack(new_kpe, axis=1))
```

```python
import functools
import math

import jax
import jax.numpy as jnp
from jax import lax
from jax.experimental import pallas as pl
from jax.experimental.pallas import tpu as pltpu

GRID_W = 64
DA_HEAD_DIM = 64
DA_V_DIM = 2 * DA_HEAD_DIM
MLA_NOPE = 64
MLA_ROPE = 32
MLA_QK_DIM = MLA_NOPE + MLA_ROPE
MLA_V_DIM = 64
PEER_N_KEYS = 128
PEER_HALF = 128
PEER_TOPK = 16
ROPE_BASE = 10000.0
EPS = 1e-6

LANE = 128
VMEM_LIMIT = 56 * 1024 * 1024
TOK_TILE = 256
Q_TILE = 1024
KEY_CHUNK = 1024
LOG2E = math.log2(math.e)
PEER_A_TILE = 256
PEER_B_TILE = 512
PEER_E_TILE = 2048
PEER_B_CHUNK = 256

BF16 = jnp.bfloat16
F32 = jnp.float32


def _params(sem, flags=None):
    return pltpu.CompilerParams(dimension_semantics=sem, vmem_limit_bytes=VMEM_LIMIT, flags=flags)


def _dot(a, b):
    return jnp.dot(a, b, preferred_element_type=F32)


def _dot_nt(a, b):
    return lax.dot_general(a, b, (((1,), (1,)), ((), ())), preferred_element_type=F32)


def _rms(x):
    return x * lax.rsqrt(jnp.mean(x * x, axis=-1, keepdims=True) + EPS)


def _modulate(x, g, shift, scale):
    return _rms(x) * g * (1.0 + scale) + shift


def _rope(x, cos, sin, half):
    lane = lax.broadcasted_iota(jnp.int32, x.shape, 1)
    first = (lane & (2 * half - 1)) < half
    partner = jnp.where(first, pltpu.roll(x, LANE - half, 1), pltpu.roll(x, half, 1))
    return x * cos + partner * sin


class _Rows:
    def __init__(self, n_p, n_b, s_lat):
        self.n_p, self.n_b, self.s_lat = n_p, n_b, s_lat
        self.n = n_p + n_b * s_lat

    def mod_row(self, i, tm):
        np_t = self.n_p // tm
        return jnp.where(i < np_t, 0, 1 + (i - np_t) // (self.s_lat // tm))

    def pos_blk(self, i, tm):
        np_t = self.n_p // tm
        return jnp.where(i < np_t, self.s_lat // tm, (i - np_t) % (self.s_lat // tm))


def _mod_spec(rows, tm, which, d):
    return pl.BlockSpec((1, 1, d), lambda i: (rows.mod_row(i, tm) * 6 + which, 0, 0))


def _rope_tables(s_lat, tm, rot_lo, rot_w, period):
    hd = rot_w // 2
    half = hd // 2
    lane = jnp.arange(LANE)
    d = (lane % period) - rot_lo
    rot = (d >= 0) & (d < rot_w)
    d = jnp.clip(d, 0, rot_w - 1)
    use_row = d < hd
    dd = d % hd
    i = dd % half
    first = dd < half
    inv = 1.0 / (ROPE_BASE ** ((2 * i).astype(F32) / hd))
    t = jnp.arange(s_lat)
    row = (t // GRID_W).astype(F32)
    col = (t % GRID_W).astype(F32)
    pos = jnp.where(use_row[None, :], row[:, None], col[:, None])
    ang = pos * inv[None, :]
    cos = jnp.where(rot[None, :], jnp.cos(ang), 1.0)
    sin = jnp.where(rot[None, :], jnp.where(first[None, :], -jnp.sin(ang), jnp.sin(ang)), 0.0)
    cos = jnp.concatenate([cos, jnp.ones((tm, LANE), F32)], axis=0)
    sin = jnp.concatenate([sin, jnp.zeros((tm, LANE), F32)], axis=0)
    return cos.astype(F32), sin.astype(F32)


def _cache_out(prev, shape, blk_tail, rows, tm, layer):
    np_t = rows.n_p // tm
    spec = pl.BlockSpec((None, None) + blk_tail, lambda i: (jnp.minimum(i, np_t - 1), layer, 0, 0))
    sds = jax.ShapeDtypeStruct(shape, F32)
    if prev is None:
        return spec, sds, [], []
    return spec, sds, [prev], [pl.BlockSpec(memory_space=pl.ANY)]


def _mod_kernel(c_ref, w_ref, b_ref, o_ref):
    c = c_ref[...]
    s = c * jax.nn.sigmoid(c)
    o_ref[...] = _dot(s.astype(BF16), w_ref[...].astype(BF16)) + b_ref[...]


def _modulation(cvecs, ada_w, ada_b):
    depth, d, d6 = ada_w.shape
    r = cvecs.shape[0]
    tn = d
    return pl.pallas_call(
        _mod_kernel,
        grid=(depth, d6 // tn),
        in_specs=[
            pl.BlockSpec((r, d), lambda l, j: (0, 0)),
            pl.BlockSpec((None, d, tn), lambda l, j: (l, 0, j)),
            pl.BlockSpec((None, 1, tn), lambda l, j: (l, 0, j)),
        ],
        out_specs=pl.BlockSpec((None, r, tn), lambda l, j: (l, 0, j)),
        out_shape=jax.ShapeDtypeStruct((depth, r, d6), F32),
        compiler_params=_params(("arbitrary", "arbitrary")),
        name="adaln_modulation",
    )(cvecs, ada_w, ada_b.reshape(depth, 1, d6))


def _da_qkv_kernel(n_prompt_tiles, n_prev, x_ref, n_ref, sh_ref, sc_ref, wq_ref, wk_ref, wv_ref, qn_ref, kn_ref,
                   cos_ref, sin_ref, *refs):
    q_ref, kb_ref, vb_ref, kf_ref, vf_ref = refs[n_prev:]
    h = _modulate(x_ref[...], n_ref[...], sh_ref[0], sc_ref[0]).astype(BF16)
    q = _dot(h, wq_ref[...])
    k = _dot(h, wk_ref[...])
    v = _dot(h, wv_ref[...])
    vb_ref[...] = v.astype(BF16)
    cos = cos_ref[...]
    sin = sin_ref[...]
    lane = lax.broadcasted_iota(jnp.int32, (x_ref.shape[0], LANE), 1)
    lo = lane < DA_HEAD_DIM

    def head_norm(t, g):
        sq = t * t
        s_lo = jnp.sum(jnp.where(lo, sq, 0.0), axis=-1, keepdims=True)
        s_hi = jnp.sum(jnp.where(lo, 0.0, sq), axis=-1, keepdims=True)
        ms = jnp.where(lo, s_lo, s_hi) * (1.0 / DA_HEAD_DIM)
        return t * lax.rsqrt(ms + EPS) * g

    scale = DA_HEAD_DIM ** -0.5 * LOG2E
    ks = []
    for j in range(q.shape[1] // LANE):
        sl = slice(j * LANE, (j + 1) * LANE)
        qj = _rope(head_norm(q[:, sl], qn_ref[...]), cos, sin, DA_HEAD_DIM // 4)
        kj = _rope(head_norm(k[:, sl], kn_ref[...]), cos, sin, DA_HEAD_DIM // 4)
        q_ref[:, sl] = (qj * scale).astype(BF16)
        kb_ref[:, sl] = kj.astype(BF16)
        ks.append(kj)

    @pl.when(pl.program_id(0) < n_prompt_tiles)
    def _():
        vf_ref[...] = v
        for j, kj in enumerate(ks):
            kf_ref[:, j * LANE:(j + 1) * LANE] = kj


def _da_qkv(x, rows, norm, modarr, wq, wk, wv, qn, kn, cos, sin, layer, n_layers, prev_k, prev_v, bp, sp):
    n, d = x.shape
    tm = TOK_TILE
    w = wq.shape[1]
    full = lambda shape: pl.BlockSpec(shape, lambda i: (0,) * len(shape))
    tok = lambda width: pl.BlockSpec((tm, width), lambda i: (i, 0))
    tab = pl.BlockSpec((tm, LANE), lambda i: (rows.pos_blk(i, tm), 0))
    k_spec, k_sds, k_in, k_in_specs = _cache_out(prev_k, (bp, n_layers, sp, w), (sp, w), rows, tm, layer)
    v_spec, v_sds, v_in, v_in_specs = _cache_out(prev_v, (bp, n_layers, sp, w), (sp, w), rows, tm, layer)
    n_prev = len(k_in) + len(v_in)
    n_in = 11
    return pl.pallas_call(
        functools.partial(_da_qkv_kernel, rows.n_p // tm, n_prev),
        grid=(n // tm,),
        in_specs=[tok(d), full((1, d)), _mod_spec(rows, tm, 0, d), _mod_spec(rows, tm, 1, d),
                  full((d, w)), full((d, w)), full((d, w)), full((1, LANE)), full((1, LANE)), tab, tab]
        + k_in_specs + v_in_specs,
        out_specs=[tok(w)] * 3 + [k_spec, v_spec],
        out_shape=[jax.ShapeDtypeStruct((n, w), BF16)] * 3 + [k_sds, v_sds],
        input_output_aliases={n_in + a: 3 + a for a in range(n_prev)},
        compiler_params=_params(("arbitrary",)),
        name="da_qkv",
    )(x, norm, modarr, modarr, wq, wk, wv, qn, kn, cos, sin, *k_in, *v_in)


def _attend(qs, ksls, vsl, kv_refs):
    n = len(qs)
    m, acc = [None] * n, [None] * n
    for k_ref, v_ref in kv_refs:
        sk = k_ref.shape[0]
        step = min(KEY_CHUNK, sk)
        for c0 in range(0, sk, step):
            v = v_ref[c0:c0 + step, vsl].astype(BF16)
            v1 = jnp.concatenate([v, jnp.ones_like(v)], axis=1)
            for a in range(n):
                s = _dot_nt(qs[a], k_ref[c0:c0 + step, ksls[a]].astype(BF16))
                mx = s.max(axis=-1, keepdims=True)
                if m[a] is None:
                    m[a] = mx
                    acc[a] = _dot(jnp.exp2((s - mx).astype(BF16)), v1)
                else:
                    m_new = jnp.maximum(m[a], mx)
                    alpha = jnp.exp2(m[a] - m_new)
                    acc[a] = alpha * acc[a] + _dot(jnp.exp2((s - m_new).astype(BF16)), v1)
                    m[a] = m_new
    return [acc[a][:, :LANE] / acc[a][:, LANE:] for a in range(n)]


def _da_attn_kernel(lam_init, n_chunks, group, lam_ref, subln_ref, q_ref, *refs):
    kv = [(refs[2 * c], refs[2 * c + 1]) for c in range(n_chunks)]
    o_ref = refs[2 * n_chunks]
    lv = lam_ref[...]
    lam = (jnp.exp(jnp.sum(lv[0:1] * lv[1:2], axis=-1, keepdims=True))
           - jnp.exp(jnp.sum(lv[2:3] * lv[3:4], axis=-1, keepdims=True)) + lam_init)
    for g in range(group):
        sl = slice(g * LANE, (g + 1) * LANE)
        q = q_ref[:, sl]
        lane = lax.broadcasted_iota(jnp.int32, q.shape, 1)
        zero = jnp.zeros_like(q)
        qs = [jnp.where(lane < DA_HEAD_DIM, q, zero), jnp.where(lane < DA_HEAD_DIM, zero, q)]
        o1, o2 = _attend(qs, [sl] * 2, sl, kv)
        o = o1 - lam * o2
        o = _rms(o) * subln_ref[...] * (1.0 - lam_init)
        o_ref[:, sl] = o.astype(o_ref.dtype)


def _mla_attn_kernel(n_chunks, group, q_ref, *refs):
    kv = [(refs[2 * c], refs[2 * c + 1]) for c in range(n_chunks)]
    o_ref = refs[2 * n_chunks]
    for g in range(group):
        sls = [slice((2 * g + half) * LANE, (2 * g + half + 1) * LANE) for half in range(2)]
        vsl = slice(g * LANE, (g + 1) * LANE)
        outs = _attend([q_ref[:, sl] for sl in sls], sls, vsl, kv)
        lane = lax.broadcasted_iota(jnp.int32, outs[0].shape, 1)
        o_ref[:, vsl] = jnp.where(lane < MLA_V_DIM, outs[0], outs[1]).astype(o_ref.dtype)


def _attention(kernel, n_pairs, group, qw, q, seq, row_off, n_b, chunks, extra_in, extra_specs, name):
    tq = min(Q_TILE, seq)
    nq = seq // tq
    q_spec = pl.BlockSpec((tq, qw * group), lambda b, hp, qi: (row_off // tq + b * nq + qi, hp))
    in_specs = list(extra_specs) + [q_spec]
    args = list(extra_in) + [q]
    for k, v, sk, off in chunks:
        kv_map = functools.partial(lambda b, hp, qi, o, s: (o // s + b, hp), o=off, s=sk)
        in_specs.append(pl.BlockSpec((sk, qw * group), kv_map))
        in_specs.append(pl.BlockSpec((sk, LANE * group), kv_map))
        args += [k, v]
    return pl.pallas_call(
        functools.partial(kernel, group),
        grid=(n_b, n_pairs // group, nq),
        in_specs=in_specs,
        out_specs=pl.BlockSpec((tq, LANE * group), lambda b, hp, qi: (b * nq + qi, hp)),
        out_shape=jax.ShapeDtypeStruct((n_b * seq, n_pairs * LANE), BF16),
        compiler_params=_params(("arbitrary", "arbitrary", "arbitrary")),
        name=name,
    )(*args)


def _out_proj_kernel(n_prompt_tiles, op_ref, os_ref, w_ref, x_ref, g_ref, y_ref):
    o = jnp.where(pl.program_id(0) < n_prompt_tiles, op_ref[...], os_ref[...])
    y_ref[...] = x_ref[...] + g_ref[0] * _dot(o, w_ref[...])


def _out_proj(o_p, o_s, w, x, rows, modarr, which):
    n, d = x.shape
    tm = TOK_TILE
    k = o_p.shape[1]
    np_t = rows.n_p // tm
    return pl.pallas_call(
        functools.partial(_out_proj_kernel, np_t),
        grid=(n // tm,),
        in_specs=[pl.BlockSpec((tm, k), lambda i: (jnp.minimum(i, np_t - 1), 0)),
                  pl.BlockSpec((tm, k), lambda i: (jnp.maximum(i - np_t, 0), 0)),
                  pl.BlockSpec((k, d), lambda i: (0, 0)),
                  pl.BlockSpec((tm, d), lambda i: (i, 0)), _mod_spec(rows, tm, which, d)],
        out_specs=pl.BlockSpec((tm, d), lambda i: (i, 0)),
        out_shape=jax.ShapeDtypeStruct((n, d), F32),
        compiler_params=_params(("arbitrary",)),
        name="out_proj",
    )(o_p, o_s, w, x, modarr)


def _head_norm128(t, g, width):
    ms = jnp.sum(t * t, axis=-1, keepdims=True) * (1.0 / width)
    return t * lax.rsqrt(ms + EPS) * g


def _mla_q_kernel(kv_lora, n_prompt_tiles, n_prev, x_ref, n_ref, sh_ref, sc_ref, wdq_ref, qan_ref, wuq_ref, qn_ref,
                  wdkv_ref, kvn_ref, cos_ref, sin_ref, *refs):
    q_ref, ckv_ref, kpe_ref, ckvp_ref, kpep_ref = refs[n_prev:]
    h = _modulate(x_ref[...], n_ref[...], sh_ref[0], sc_ref[0]).astype(BF16)
    cq = (_rms(_dot(h, wdq_ref[...])) * qan_ref[...]).astype(BF16)
    q = _dot(cq, wuq_ref[...])
    cos = cos_ref[...]
    sin = sin_ref[...]
    scale = MLA_QK_DIM ** -0.5 * LOG2E
    for j in range(q.shape[1] // LANE):
        sl = slice(j * LANE, (j + 1) * LANE)
        qj = _rope(_head_norm128(q[:, sl], qn_ref[...], MLA_QK_DIM), cos, sin, MLA_ROPE // 4)
        q_ref[:, sl] = (qj * scale).astype(BF16)
    ckvf = _dot(h, wdkv_ref[...])
    ckv = _rms(ckvf[:, :kv_lora]) * kvn_ref[...]
    ckv_ref[...] = ckv
    kpe_ref[...] = ckvf[:, kv_lora:]

    @pl.when(pl.program_id(0) < n_prompt_tiles)
    def _():
        ckvp_ref[...] = ckv
        kpep_ref[...] = ckvf[:, kv_lora + MLA_NOPE:kv_lora + MLA_QK_DIM]


def _mla_q(x, rows, norm, modarr, wdq, qan, wuq, qn, wdkv, kvn, cos, sin, layer, n_layers, prev_c, prev_p, bp, sp):
    n, d = x.shape
    tm = TOK_TILE
    ql = wdq.shape[1]
    qw = wuq.shape[1]
    kv_lora = kvn.shape[1]
    full = lambda shape: pl.BlockSpec(shape, lambda i: (0,) * len(shape))
    tok = lambda width: pl.BlockSpec((tm, width), lambda i: (i, 0))
    tab = pl.BlockSpec((tm, LANE), lambda i: (rows.pos_blk(i, tm), 0))
    c_spec, c_sds, c_in, c_in_specs = _cache_out(prev_c, (bp, n_layers, sp, kv_lora), (sp, kv_lora), rows, tm, layer)
    p_spec, p_sds, p_in, p_in_specs = _cache_out(prev_p, (bp, n_layers, sp, MLA_ROPE), (sp, MLA_ROPE), rows, tm, layer)
    n_prev = len(c_in) + len(p_in)
    n_in = 12
    return pl.pallas_call(
        functools.partial(_mla_q_kernel, kv_lora, rows.n_p // tm, n_prev),
        grid=(n // tm,),
        in_specs=[tok(d), full((1, d)), _mod_spec(rows, tm, 0, d), _mod_spec(rows, tm, 1, d),
                  full((d, ql)), full((1, ql)), full((ql, qw)), full((1, LANE)),
                  full((d, kv_lora + LANE)), full((1, kv_lora)), tab, tab] + c_in_specs + p_in_specs,
        out_specs=[tok(qw), tok(kv_lora), tok(LANE), c_spec, p_spec],
        out_shape=[jax.ShapeDtypeStruct((n, qw), BF16), jax.ShapeDtypeStruct((n, kv_lora), F32),
                   jax.ShapeDtypeStruct((n, LANE), F32), c_sds, p_sds],
        input_output_aliases={n_in + a: 3 + a for a in range(n_prev)},
        compiler_params=_params(("arbitrary",)),
        name="mla_q_ckv",
    )(x, norm, modarr, modarr, wdq, qan, wuq, qn, wdkv, kvn, cos, sin, *c_in, *p_in)


def _mla_kv_kernel(ckv_ref, kpe_ref, wuk_ref, wuv_ref, kn_ref, cos_ref, sin_ref, k_ref, v_ref):
    c = ckv_ref[...].astype(BF16)
    kn = _dot(c, wuk_ref[...])
    v_ref[...] = _dot(c, wuv_ref[...]).astype(BF16)
    kpe = kpe_ref[...]
    cos = cos_ref[...]
    sin = sin_ref[...]
    for j in range(kn.shape[1] // LANE):
        sl = slice(j * LANE, (j + 1) * LANE)
        kj = _rope(_head_norm128(kn[:, sl] + kpe, kn_ref[...], MLA_QK_DIM), cos, sin, MLA_ROPE // 4)
        k_ref[:, sl] = kj.astype(BF16)


def _mla_kv(ckv, kpe, wuk, wuv, kn, cos, sin, pos_blk):
    n, kv_lora = ckv.shape
    tm = TOK_TILE
    kw = wuk.shape[1]
    vw = wuv.shape[1]
    full = lambda shape: pl.BlockSpec(shape, lambda i: (0,) * len(shape))
    tok = lambda width: pl.BlockSpec((tm, width), lambda i: (i, 0))
    tab = pl.BlockSpec((tm, LANE), lambda i: (pos_blk(i), 0))
    return pl.pallas_call(
        _mla_kv_kernel,
        grid=(n // tm,),
        in_specs=[tok(kv_lora), tok(LANE), full((kv_lora, kw)), full((kv_lora, vw)), full((1, LANE)), tab, tab],
        out_specs=[tok(kw), tok(vw)],
        out_shape=[jax.ShapeDtypeStruct((n, kw), BF16), jax.ShapeDtypeStruct((n, vw), BF16)],
        compiler_params=_params(("arbitrary",)),
        name="mla_kv_expand",
    )(ckv, kpe, wuk, wuv, kn, cos, sin)


def _sort_pairs(n):
    pairs = []
    p = 1
    while p < n:
        k = p
        while k >= 1:
            for j in range(k % p, n - k, 2 * k):
                for i in range(min(k, n - j - k)):
                    if (i + j) // (2 * p) == (i + j + k) // (2 * p):
                        pairs.append((i + j, i + j + k))
            k //= 2
        p *= 2
    return pairs


def _cmpx(x, i, j):
    a, b = x[i], x[j]
    if b is None:
        return
    if a is None:
        x[i], x[j] = b, None
        return
    x[i], x[j] = jnp.maximum(a, b), jnp.minimum(a, b)


def _top_sorted(x):
    k = PEER_TOPK
    x = list(x)
    for i, j in _sort_pairs(k):
        _cmpx(x, i, j)
    shift = 4
    while shift >= 1:
        y = []
        for i in range(k):
            other = x[k - 1 - i]
            other = None if other is None else pltpu.roll(other, 8 - shift, 0)
            if x[i] is None:
                y.append(other)
            elif other is None:
                y.append(x[i])
            else:
                y.append(jnp.maximum(x[i], other))
        d = k // 2
        while d >= 1:
            for i in range(k):
                if (i & d) == 0:
                    _cmpx(y, i, i + d)
            d //= 2
        x = y
        shift //= 2
    return x


def _pair_candidates(a_ref, b_ref, scale):
    k = PEER_TOPK
    b_lo = b_ref[0:8, :]
    sub = lax.broadcasted_iota(jnp.int32, b_lo.shape, 0)
    row = lambda r: a_ref[r:r + 1, :] * scale
    out = [row(0) * b_lo, row(0) * b_ref[8:16, :]]
    for r in range(1, 8):
        n_r = k // (r + 1)
        c = row(r) * b_lo
        out.append(c if n_r >= 8 else jnp.where(sub < n_r, c, -1.0))
    out.append((a_ref[8:16, :] * scale) * b_ref[0:1, :])
    return out + [None] * (k - len(out))


def _peer_a_kernel(n_heads, x_ref, n_ref, sh_ref, sc_ref, wqt_ref, keys_ref,
                   ht_ref, e1_ref, e0n_ref, tau_ref, qt_sc, e_sc, top_sc):
    k = PEER_TOPK
    h = _modulate(x_ref[...], n_ref[...], sh_ref[0], sc_ref[0])
    ht = h.T.astype(BF16)
    ht_ref[...] = ht
    qt_sc[...] = _dot(wqt_ref[...], ht)

    for hc in range(2 * n_heads):
        qb = qt_sc[hc * PEER_HALF:(hc + 1) * PEER_HALF, :]
        s = _dot(keys_ref[hc], qb.astype(BF16))
        e_sc[hc] = jnp.exp(s - jnp.max(s, axis=0, keepdims=True))

    def per_set(hc, carry):
        e = e_sc[hc]
        top = _top_sorted([e[g * 8:(g + 1) * 8, :] for g in range(PEER_N_KEYS // 8)])
        for r in range(k):
            top_sc[hc, r:r + 1, :] = top[r][0:1, :]
        return carry

    lax.fori_loop(0, 2 * n_heads, per_set, 0)

    def per_head(hh, carry):
        a_ref = top_sc.at[2 * hh]
        b_ref = top_sc.at[2 * hh + 1]
        top = _top_sorted(_pair_candidates(a_ref, b_ref, 1.0))
        z = top[0][0:1, :]
        for r in range(1, k):
            z = z + top[r][0:1, :]
        rz = 1.0 / z
        e0n_ref[hh] = e_sc[2 * hh] * rz
        e1_ref[hh] = e_sc[2 * hh + 1]
        topn = _top_sorted(_pair_candidates(a_ref, b_ref, rz))
        tau_ref[pl.ds(hh, 1), :] = topn[k - 1][0:1, :]
        return carry

    lax.fori_loop(0, n_heads, per_head, 0)


def _peer_a(x, rows, norm, modarr, wqt, keys):
    n, d = x.shape
    t = PEER_A_TILE
    n_sets = keys.shape[0]
    n_heads = n_sets // 2
    full = lambda shape: pl.BlockSpec(shape, lambda i: (0,) * len(shape))
    return pl.pallas_call(
        functools.partial(_peer_a_kernel, n_heads),
        grid=(n // t,),
        in_specs=[pl.BlockSpec((t, d), lambda i: (i, 0)), full((1, d)),
                  _mod_spec(rows, t, 3, d), _mod_spec(rows, t, 4, d),
                  full(wqt.shape), full(keys.shape)],
        out_specs=[pl.BlockSpec((d, t), lambda i: (0, i)),
                   pl.BlockSpec((n_heads, PEER_N_KEYS, t), lambda i: (0, 0, i)),
                   pl.BlockSpec((n_heads, PEER_N_KEYS, t), lambda i: (0, 0, i)),
                   pl.BlockSpec((n_heads, t), lambda i: (0, i))],
        out_shape=[jax.ShapeDtypeStruct((d, n), BF16),
                   jax.ShapeDtypeStruct((n_heads, PEER_N_KEYS, n), F32),
                   jax.ShapeDtypeStruct((n_heads, PEER_N_KEYS, n), F32),
                   jax.ShapeDtypeStruct((n_heads, n), F32)],
        scratch_shapes=[pltpu.VMEM((n_sets * PEER_HALF, t), F32),
                        pltpu.VMEM((n_sets, PEER_N_KEYS, t), F32),
                        pltpu.VMEM((n_sets, PEER_TOPK, t), F32)],
        compiler_params=_params(("arbitrary",)),
        name="peer_select",
    )(x, norm, modarr, modarr, wqt, keys)


def _peer_b_kernel(n_heads, n_chunks, ht_ref, u_ref, vt_ref, e1_ref, e0n_ref, tau_ref, x_ref, g_ref, y_ref,
                   acc_sc, *chunk_sc):
    at_sc, w_sc = chunk_sc[:n_chunks], chunk_sc[n_chunks:]
    e = pl.program_id(1)

    @pl.when(e == 0)
    def _():
        acc_sc[...] = jnp.zeros_like(acc_sc)

    ht = ht_ref[...]
    t = ht.shape[1]
    ch = PEER_B_CHUNK
    for c in range(n_chunks):
        at_sc[c][...] = _dot(u_ref[c * ch:(c + 1) * ch, :], ht)
    for c in range(n_chunks):
        for r0 in range(0, ch, 64):
            ii, j0 = divmod(c * ch + r0, PEER_N_KEYS)
            for l0 in range(0, t, LANE):
                ls = slice(l0, l0 + LANE)
                gate = None
                for hh in range(n_heads):
                    p = e1_ref[hh, j0:j0 + 64, ls] * e0n_ref[hh, ii:ii + 1, ls]
                    g = jnp.where(p >= tau_ref[hh:hh + 1, ls], p, 0.0)
                    gate = g if gate is None else gate + g
                w_sc[c][r0:r0 + 64, ls] = (gate * jax.nn.gelu(at_sc[c][r0:r0 + 64, ls])).astype(BF16)
        acc_sc[...] += _dot(vt_ref[:, c * ch:(c + 1) * ch], w_sc[c][...])

    @pl.when(e == pl.num_programs(1) - 1)
    def _():
        y_ref[...] = x_ref[...] + g_ref[0] * acc_sc[...].T


def _peer_b(ht, u, vt, layer, e1, e0n, tau, x, rows, modarr):
    n, d = x.shape
    t = PEER_B_TILE
    te = PEER_E_TILE
    n_exp = u.shape[1]
    n_heads = e1.shape[0]
    ti = te // PEER_N_KEYS
    n_chunks = te // PEER_B_CHUNK
    return pl.pallas_call(
        functools.partial(_peer_b_kernel, n_heads, n_chunks),
        grid=(n // t, n_exp // te),
        in_specs=[pl.BlockSpec((d, t), lambda i, e: (0, i)),
                  pl.BlockSpec((None, te, d), lambda i, e: (layer, e, 0)),
                  pl.BlockSpec((None, d, te), lambda i, e: (layer, 0, e)),
                  pl.BlockSpec((n_heads, PEER_N_KEYS, t), lambda i, e: (0, 0, i)),
                  pl.BlockSpec((n_heads, ti, t), lambda i, e: (0, e, i)),
                  pl.BlockSpec((n_heads, t), lambda i, e: (0, i)),
                  pl.BlockSpec((t, d), lambda i, e: (i, 0)),
                  pl.BlockSpec((1, 1, d), lambda i, e: (rows.mod_row(i, t) * 6 + 5, 0, 0))],
        out_specs=pl.BlockSpec((t, d), lambda i, e: (i, 0)),
        out_shape=jax.ShapeDtypeStruct((n, d), F32),
        scratch_shapes=([pltpu.VMEM((d, t), F32)] + [pltpu.VMEM((PEER_B_CHUNK, t), F32)] * n_chunks
                        + [pltpu.VMEM((PEER_B_CHUNK, t), BF16)] * n_chunks),
        compiler_params=_params(("arbitrary", "arbitrary")),
        name="peer_experts",
    )(ht, u, vt, e1, e0n, tau, x, modarr)


def _pad_heads(w, n_heads, width):
    k = w.shape[0]
    w = w.reshape(k, n_heads, width)
    return jnp.pad(w, ((0, 0), (0, 0), (0, LANE - width))).reshape(k, n_heads * LANE)


def _pad_lanes(g, lo=0):
    g = g.reshape(1, -1).astype(F32)
    return jnp.pad(g, ((0, 0), (lo, LANE - lo - g.shape[1])))


def kernel(x_prompt, x_sample, cache_da_k, cache_da_v, cache_mla_ckv, cache_mla_kpe, c, c_ctx, norm1, norm2, ada_w, ada_b, da_wq, da_wk, da_wv, da_wo, da_q_norm, da_k_norm, da_lq1, da_lk1, da_lq2, da_lk2, da_subln, mla_wdq, mla_qa_norm, mla_wuq, mla_wdkv, mla_kv_norm, mla_wukv, mla_q_norm, mla_k_norm, mla_wo, peer_wq, peer_keys, peer_u, peer_v):
    bp, sp, d = x_prompt.shape
    bs, ss, _ = x_sample.shape
    depth = norm1.shape[0]
    past = cache_da_k.shape[2]
    n_p = bp * sp
    rows = _Rows(n_p, bs, ss)
    tm = TOK_TILE
    assert sp % tm == 0 and ss % tm == 0 and n_p % ss == 0 and past == sp
    assert n_p % PEER_B_TILE == 0 and ss % PEER_B_TILE == 0

    da_heads = da_wv.shape[2] // DA_V_DIM
    mla_heads = mla_wo.shape[1] // MLA_V_DIM
    kv_lora = mla_kv_norm.shape[1]
    peer_heads = peer_keys.shape[1]

    x = jnp.concatenate([x_prompt.reshape(n_p, d), x_sample.reshape(bs * ss, d)], axis=0)

    n_cond = 1 + bs
    r_pad = -(-n_cond // 8) * 8
    cvecs = jnp.concatenate([c_ctx[None, :], c, jnp.zeros((r_pad - n_cond, d), F32)], axis=0)
    mods = _modulation(cvecs, ada_w, ada_b).reshape(depth, r_pad * 6, 1, d)

    da_cos, da_sin = _rope_tables(ss, tm, 0, DA_HEAD_DIM, DA_HEAD_DIM)
    mla_cos, mla_sin = _rope_tables(ss, tm, MLA_NOPE, MLA_ROPE, LANE)
    ident = ss // tm

    assert tm == sp
    n_da = (depth + 1) // 2
    n_mla = depth // 2
    da_k = da_v = mla_c = mla_p = None
    u_all = peer_u.astype(BF16)
    vt_all = jnp.swapaxes(peer_v, 1, 2).astype(BF16)
    for l in range(depth):
        modarr = mods[l]
        j = l // 2
        if l % 2 == 0:
            lam_init = 0.8 - 0.6 * math.exp(-0.3 * l)
            q, kb, vb, da_k, da_v = _da_qkv(
                x, rows, norm1[l][None, :], modarr,
                da_wq[j].astype(BF16), da_wk[j].astype(BF16), da_wv[j].astype(BF16),
                jnp.tile(da_q_norm[j][None, :], (1, 2)), jnp.tile(da_k_norm[j][None, :], (1, 2)), da_cos, da_sin,
                j, n_da, da_k, da_v, bp, sp)
            lam_vecs = jnp.stack([da_lq1[j], da_lk1[j], da_lq2[j], da_lk2[j]]).astype(F32)
            extra = [lam_vecs, da_subln[j][None, :]]
            extra_specs = [pl.BlockSpec(lam_vecs.shape, lambda b, hp, qi: (0, 0)),
                           pl.BlockSpec((1, DA_V_DIM), lambda b, hp, qi: (0, 0))]
            ck = cache_da_k[:, j].reshape(bs * past, 2 * da_heads * DA_HEAD_DIM)
            cv = cache_da_v[:, j].reshape(bs * past, da_heads * DA_V_DIM)
            o_p = _attention(functools.partial(_da_attn_kernel, lam_init, 1), da_heads, da_heads, LANE, q, sp, 0, bp,
                             [(kb, vb, sp, 0)], extra, extra_specs, "da_attn_prompt")
            o_s = _attention(functools.partial(_da_attn_kernel, lam_init, 2), da_heads, 1, LANE, q, ss, n_p, bs,
                             [(kb, vb, ss, n_p), (ck, cv, past, 0)], extra, extra_specs, "da_attn_latent")
            wo = da_wo[j].astype(BF16)
        else:
            wuq = _pad_heads(mla_wuq[j], mla_heads, MLA_QK_DIM).astype(BF16)
            wdkv = jnp.concatenate([
                mla_wdkv[j][:, :kv_lora],
                jnp.pad(mla_wdkv[j][:, kv_lora:], ((0, 0), (MLA_NOPE, LANE - MLA_QK_DIM)))], axis=1).astype(BF16)
            wukv = mla_wukv[j].reshape(kv_lora, mla_heads, MLA_NOPE + MLA_V_DIM)
            wuk = jnp.pad(wukv[:, :, :MLA_NOPE], ((0, 0), (0, 0), (0, LANE - MLA_NOPE)))
            wuk = wuk.reshape(kv_lora, mla_heads * LANE).astype(BF16)
            wuv = wukv[:, :, MLA_NOPE:].reshape(kv_lora, mla_heads * MLA_V_DIM).astype(BF16)
            q, ckv, kpe, mla_c, mla_p = _mla_q(
                x, rows, norm1[l][None, :], modarr, mla_wdq[j].astype(BF16), mla_qa_norm[j][None, :], wuq,
                _pad_lanes(mla_q_norm[j]), wdkv, mla_kv_norm[j][None, :], mla_cos, mla_sin,
                j, n_mla, mla_c, mla_p, bp, sp)
            kn = _pad_lanes(mla_k_norm[j])
            k, v = _mla_kv(ckv, kpe, wuk, wuv, kn, mla_cos, mla_sin, lambda i: rows.pos_blk(i, tm))
            c_ckv = cache_mla_ckv[:, j].reshape(bs * past, kv_lora)
            c_kpe = jnp.pad(cache_mla_kpe[:, j].reshape(bs * past, MLA_ROPE), ((0, 0), (MLA_NOPE, LANE - MLA_QK_DIM)))
            ck, cv = _mla_kv(c_ckv, c_kpe, wuk, wuv, kn, mla_cos, mla_sin, lambda i: ident)
            o_p = _attention(functools.partial(_mla_attn_kernel, 1), mla_heads // 2, mla_heads // 2, 2 * LANE, q, sp, 0, bp,
                             [(k, v, sp, 0)], [], [], "mla_attn_prompt")
            o_s = _attention(functools.partial(_mla_attn_kernel, 2), mla_heads // 2, 1, 2 * LANE, q, ss, n_p, bs,
                             [(k, v, ss, n_p), (ck, cv, past, 0)], [], [], "mla_attn_latent")
            wo = mla_wo[j].astype(BF16)
        x = _out_proj(o_p, o_s, wo, x, rows, modarr, 2)

        wqt = peer_wq[l].T.astype(BF16)
        keys = peer_keys[l].reshape(2 * peer_heads, PEER_N_KEYS, PEER_HALF).astype(BF16)
        ht, e1, e0n, tau = _peer_a(x, rows, norm2[l][None, :], modarr, wqt, keys)
        x = _peer_b(ht, u_all, vt_all, l, e1, e0n, tau, x, rows, modarr)

    return (x[:n_p].reshape(bp, sp, d), x[n_p:].reshape(bs, ss, d),
            da_k.reshape(bp, n_da, sp, 2 * da_heads, DA_HEAD_DIM), da_v.reshape(bp, n_da, sp, da_heads, DA_V_DIM),
            mla_c, mla_p)
```

```python
import functools
import math

import jax
import jax.numpy as jnp
from jax import lax
from jax.experimental import pallas as pl
from jax.experimental.pallas import tpu as pltpu

GRID_W = 64
DA_HEAD_DIM = 64
DA_V_DIM = 2 * DA_HEAD_DIM
MLA_NOPE = 64
MLA_ROPE = 32
MLA_QK_DIM = MLA_NOPE + MLA_ROPE
MLA_V_DIM = 64
PEER_N_KEYS = 128
PEER_HALF = 128
PEER_TOPK = 16
ROPE_BASE = 10000.0
EPS = 1e-6

LANE = 128
VMEM_LIMIT = 56 * 1024 * 1024
TOK_TILE = 256
Q_TILE = 1024
KEY_CHUNK = 1024
LOG2E = math.log2(math.e)
PEER_A_TILE = 256
PEER_B_TILE = 512
PEER_E_TILE = 2048
PEER_B_CHUNK = 256

BF16 = jnp.bfloat16
F32 = jnp.float32


def _params(sem, flags=None):
    return pltpu.CompilerParams(dimension_semantics=sem, vmem_limit_bytes=VMEM_LIMIT, flags=flags)


def _dot(a, b):
    return jnp.dot(a, b, preferred_element_type=F32)


def _dot_nt(a, b):
    return lax.dot_general(a, b, (((1,), (1,)), ((), ())), preferred_element_type=F32)


def _rms(x):
    return x * lax.rsqrt(jnp.mean(x * x, axis=-1, keepdims=True) + EPS)


def _modulate(x, g, shift, scale):
    return _rms(x) * g * (1.0 + scale) + shift


def _rope(x, cos, sin, half):
    lane = lax.broadcasted_iota(jnp.int32, x.shape, 1)
    first = (lane & (2 * half - 1)) < half
    partner = jnp.where(first, pltpu.roll(x, LANE - half, 1), pltpu.roll(x, half, 1))
    return x * cos + partner * sin


class _Rows:
    def __init__(self, n_p, n_b, s_lat):
        self.n_p, self.n_b, self.s_lat = n_p, n_b, s_lat
        self.n = n_p + n_b * s_lat

    def mod_row(self, i, tm):
        np_t = self.n_p // tm
        return jnp.where(i < np_t, 0, 1 + (i - np_t) // (self.s_lat // tm))

    def pos_blk(self, i, tm):
        np_t = self.n_p // tm
        return jnp.where(i < np_t, self.s_lat // tm, (i - np_t) % (self.s_lat // tm))


def _mod_spec(rows, tm, which, d):
    return pl.BlockSpec((1, 1, d), lambda i: (rows.mod_row(i, tm) * 6 + which, 0, 0))


def _rope_tables(s_lat, tm, rot_lo, rot_w, period):
    hd = rot_w // 2
    half = hd // 2
    lane = jnp.arange(LANE)
    d = (lane % period) - rot_lo
    rot = (d >= 0) & (d < rot_w)
    d = jnp.clip(d, 0, rot_w - 1)
    use_row = d < hd
    dd = d % hd
    i = dd % half
    first = dd < half
    inv = 1.0 / (ROPE_BASE ** ((2 * i).astype(F32) / hd))
    t = jnp.arange(s_lat)
    row = (t // GRID_W).astype(F32)
    col = (t % GRID_W).astype(F32)
    pos = jnp.where(use_row[None, :], row[:, None], col[:, None])
    ang = pos * inv[None, :]
    cos = jnp.where(rot[None, :], jnp.cos(ang), 1.0)
    sin = jnp.where(rot[None, :], jnp.where(first[None, :], -jnp.sin(ang), jnp.sin(ang)), 0.0)
    cos = jnp.concatenate([cos, jnp.ones((tm, LANE), F32)], axis=0)
    sin = jnp.concatenate([sin, jnp.zeros((tm, LANE), F32)], axis=0)
    return cos.astype(F32), sin.astype(F32)


def _cache_out(prev, shape, blk_tail, rows, tm, layer):
    np_t = rows.n_p // tm
    spec = pl.BlockSpec((None, None) + blk_tail, lambda i: (jnp.minimum(i, np_t - 1), layer, 0, 0))
    sds = jax.ShapeDtypeStruct(shape, F32)
    if prev is None:
        return spec, sds, [], []
    return spec, sds, [prev], [pl.BlockSpec(memory_space=pl.ANY)]


def _mod_kernel(c_ref, w_ref, b_ref, o_ref):
    c = c_ref[...]
    s = c * jax.nn.sigmoid(c)
    o_ref[...] = _dot(s.astype(BF16), w_ref[...].astype(BF16)) + b_ref[...]


def _modulation(cvecs, ada_w, ada_b):
    depth, d, d6 = ada_w.shape
    r = cvecs.shape[0]
    tn = d
    return pl.pallas_call(
        _mod_kernel,
        grid=(depth, d6 // tn),
        in_specs=[
            pl.BlockSpec((r, d), lambda l, j: (0, 0)),
            pl.BlockSpec((None, d, tn), lambda l, j: (l, 0, j)),
            pl.BlockSpec((None, 1, tn), lambda l, j: (l, 0, j)),
        ],
        out_specs=pl.BlockSpec((None, r, tn), lambda l, j: (l, 0, j)),
        out_shape=jax.ShapeDtypeStruct((depth, r, d6), F32),
        compiler_params=_params(("arbitrary", "arbitrary")),
        name="adaln_modulation",
    )(cvecs, ada_w, ada_b.reshape(depth, 1, d6))


def _da_qkv_kernel(n_prompt_tiles, n_prev, x_ref, n_ref, sh_ref, sc_ref, wq_ref, wk_ref, wv_ref, qn_ref, kn_ref,
                   cos_ref, sin_ref, *refs):
    q_ref, kb_ref, vb_ref, kf_ref, vf_ref = refs[n_prev:]
    h = _modulate(x_ref[...], n_ref[...], sh_ref[0], sc_ref[0]).astype(BF16)
    q = _dot(h, wq_ref[...])
    k = _dot(h, wk_ref[...])
    v = _dot(h, wv_ref[...])
    vb_ref[...] = v.astype(BF16)
    cos = cos_ref[...]
    sin = sin_ref[...]
    lane = lax.broadcasted_iota(jnp.int32, (x_ref.shape[0], LANE), 1)
    lo = lane < DA_HEAD_DIM

    def head_norm(t, g):
        sq = t * t
        s_lo = jnp.sum(jnp.where(lo, sq, 0.0), axis=-1, keepdims=True)
        s_hi = jnp.sum(jnp.where(lo, 0.0, sq), axis=-1, keepdims=True)
        ms = jnp.where(lo, s_lo, s_hi) * (1.0 / DA_HEAD_DIM)
        return t * lax.rsqrt(ms + EPS) * g

    scale = DA_HEAD_DIM ** -0.5 * LOG2E
    ks = []
    for j in range(q.shape[1] // LANE):
        sl = slice(j * LANE, (j + 1) * LANE)
        qj = _rope(head_norm(q[:, sl], qn_ref[...]), cos, sin, DA_HEAD_DIM // 4)
        kj = _rope(head_norm(k[:, sl], kn_ref[...]), cos, sin, DA_HEAD_DIM // 4)
        q_ref[:, sl] = (qj * scale).astype(BF16)
        kb_ref[:, sl] = kj.astype(BF16)
        ks.append(kj)

    @pl.when(pl.program_id(0) < n_prompt_tiles)
    def _():
        vf_ref[...] = v
        for j, kj in enumerate(ks):
            kf_ref[:, j * LANE:(j + 1) * LANE] = kj


def _da_qkv(x, rows, norm, modarr, wq, wk, wv, qn, kn, cos, sin, layer, n_layers, prev_k, prev_v, bp, sp):
    n, d = x.shape
    tm = TOK_TILE
    w = wq.shape[1]
    full = lambda shape: pl.BlockSpec(shape, lambda i: (0,) * len(shape))
    tok = lambda width: pl.BlockSpec((tm, width), lambda i: (i, 0))
    tab = pl.BlockSpec((tm, LANE), lambda i: (rows.pos_blk(i, tm), 0))
    k_spec, k_sds, k_in, k_in_specs = _cache_out(prev_k, (bp, n_layers, sp, w), (sp, w), rows, tm, layer)
    v_spec, v_sds, v_in, v_in_specs = _cache_out(prev_v, (bp, n_layers, sp, w), (sp, w), rows, tm, layer)
    n_prev = len(k_in) + len(v_in)
    n_in = 11
    return pl.pallas_call(
        functools.partial(_da_qkv_kernel, rows.n_p // tm, n_prev),
        grid=(n // tm,),
        in_specs=[tok(d), full((1, d)), _mod_spec(rows, tm, 0, d), _mod_spec(rows, tm, 1, d),
                  full((d, w)), full((d, w)), full((d, w)), full((1, LANE)), full((1, LANE)), tab, tab]
        + k_in_specs + v_in_specs,
        out_specs=[tok(w)] * 3 + [k_spec, v_spec],
        out_shape=[jax.ShapeDtypeStruct((n, w), BF16)] * 3 + [k_sds, v_sds],
        input_output_aliases={n_in + a: 3 + a for a in range(n_prev)},
        compiler_params=_params(("arbitrary",)),
        name="da_qkv",
    )(x, norm, modarr, modarr, wq, wk, wv, qn, kn, cos, sin, *k_in, *v_in)


def _attend(qs, ksls, vsl, kv_refs):
    n = len(qs)
    m, acc = [None] * n, [None] * n
    for k_ref, v_ref in kv_refs:
        sk = k_ref.shape[0]
        step = min(KEY_CHUNK, sk)
        for c0 in range(0, sk, step):
            v = v_ref[c0:c0 + step, vsl].astype(BF16)
            v1 = jnp.concatenate([v, jnp.ones_like(v)], axis=1)
            for a in range(n):
                s = _dot_nt(qs[a], k_ref[c0:c0 + step, ksls[a]].astype(BF16))
                mx = s.max(axis=-1, keepdims=True)
                if m[a] is None:
                    m[a] = mx
                    acc[a] = _dot(jnp.exp2((s - mx).astype(BF16)), v1)
                else:
                    m_new = jnp.maximum(m[a], mx)
                    alpha = jnp.exp2(m[a] - m_new)
                    acc[a] = alpha * acc[a] + _dot(jnp.exp2((s - m_new).astype(BF16)), v1)
                    m[a] = m_new
    return [acc[a][:, :LANE] / acc[a][:, LANE:] for a in range(n)]


def _da_attn_kernel(lam_init, n_chunks, group, lam_ref, subln_ref, q_ref, *refs):
    kv = [(refs[2 * c], refs[2 * c + 1]) for c in range(n_chunks)]
    o_ref = refs[2 * n_chunks]
    lv = lam_ref[...]
    lam = (jnp.exp(jnp.sum(lv[0:1] * lv[1:2], axis=-1, keepdims=True))
           - jnp.exp(jnp.sum(lv[2:3] * lv[3:4], axis=-1, keepdims=True)) + lam_init)
    for g in range(group):
        sl = slice(g * LANE, (g + 1) * LANE)
        q = q_ref[:, sl]
        lane = lax.broadcasted_iota(jnp.int32, q.shape, 1)
        zero = jnp.zeros_like(q)
        qs = [jnp.where(lane < DA_HEAD_DIM, q, zero), jnp.where(lane < DA_HEAD_DIM, zero, q)]
        o1, o2 = _attend(qs, [sl] * 2, sl, kv)
        o = o1 - lam * o2
        o = _rms(o) * subln_ref[...] * (1.0 - lam_init)
        o_ref[:, sl] = o.astype(o_ref.dtype)


def _mla_attn_kernel(n_chunks, group, q_ref, *refs):
    kv = [(refs[2 * c], refs[2 * c + 1]) for c in range(n_chunks)]
    o_ref = refs[2 * n_chunks]
    for g in range(group):
        sls = [slice((2 * g + half) * LANE, (2 * g + half + 1) * LANE) for half in range(2)]
        vsl = slice(g * LANE, (g + 1) * LANE)
        outs = _attend([q_ref[:, sl] for sl in sls], sls, vsl, kv)
        lane = lax.broadcasted_iota(jnp.int32, outs[0].shape, 1)
        o_ref[:, vsl] = jnp.where(lane < MLA_V_DIM, outs[0], outs[1]).astype(o_ref.dtype)


def _attention(kernel, n_pairs, group, qw, q, seq, row_off, n_b, chunks, extra_in, extra_specs, name):
    tq = min(Q_TILE, seq)
    nq = seq // tq
    q_spec = pl.BlockSpec((tq, qw * group), lambda b, hp, qi: (row_off // tq + b * nq + qi, hp))
    in_specs = list(extra_specs) + [q_spec]
    args = list(extra_in) + [q]
    for k, v, sk, off in chunks:
        kv_map = functools.partial(lambda b, hp, qi, o, s: (o // s + b, hp), o=off, s=sk)
        in_specs.append(pl.BlockSpec((sk, qw * group), kv_map))
        in_specs.append(pl.BlockSpec((sk, LANE * group), kv_map))
        args += [k, v]
    return pl.pallas_call(
        functools.partial(kernel, group),
        grid=(n_b, n_pairs // group, nq),
        in_specs=in_specs,
        out_specs=pl.BlockSpec((tq, LANE * group), lambda b, hp, qi: (b * nq + qi, hp)),
        out_shape=jax.ShapeDtypeStruct((n_b * seq, n_pairs * LANE), BF16),
        compiler_params=_params(("arbitrary", "arbitrary", "arbitrary")),
        name=name,
    )(*args)


def _out_proj_kernel(n_prompt_tiles, op_ref, os_ref, w_ref, x_ref, g_ref, y_ref):
    o = jnp.where(pl.program_id(0) < n_prompt_tiles, op_ref[...], os_ref[...])
    y_ref[...] = x_ref[...] + g_ref[0] * _dot(o, w_ref[...])


def _out_proj(o_p, o_s, w, x, rows, modarr, which):
    n, d = x.shape
    tm = TOK_TILE
    k = o_p.shape[1]
    np_t = rows.n_p // tm
    return pl.pallas_call(
        functools.partial(_out_proj_kernel, np_t),
        grid=(n // tm,),
        in_specs=[pl.BlockSpec((tm, k), lambda i: (jnp.minimum(i, np_t - 1), 0)),
                  pl.BlockSpec((tm, k), lambda i: (jnp.maximum(i - np_t, 0), 0)),
                  pl.BlockSpec((k, d), lambda i: (0, 0)),
                  pl.BlockSpec((tm, d), lambda i: (i, 0)), _mod_spec(rows, tm, which, d)],
        out_specs=pl.BlockSpec((tm, d), lambda i: (i, 0)),
        out_shape=jax.ShapeDtypeStruct((n, d), F32),
        compiler_params=_params(("arbitrary",)),
        name="out_proj",
    )(o_p, o_s, w, x, modarr)


def _head_norm128(t, g, width):
    ms = jnp.sum(t * t, axis=-1, keepdims=True) * (1.0 / width)
    return t * lax.rsqrt(ms + EPS) * g


def _mla_q_kernel(kv_lora, n_prompt_tiles, n_prev, x_ref, n_ref, sh_ref, sc_ref, wdq_ref, qan_ref, wuq_ref, qn_ref,
                  wdkv_ref, kvn_ref, cos_ref, sin_ref, *refs):
    q_ref, ckv_ref, kpe_ref, ckvp_ref, kpep_ref = refs[n_prev:]
    h = _modulate(x_ref[...], n_ref[...], sh_ref[0], sc_ref[0]).astype(BF16)
    cq = (_rms(_dot(h, wdq_ref[...])) * qan_ref[...]).astype(BF16)
    q = _dot(cq, wuq_ref[...])
    cos = cos_ref[...]
    sin = sin_ref[...]
    scale = MLA_QK_DIM ** -0.5 * LOG2E
    for j in range(q.shape[1] // LANE):
        sl = slice(j * LANE, (j + 1) * LANE)
        qj = _rope(_head_norm128(q[:, sl], qn_ref[...], MLA_QK_DIM), cos, sin, MLA_ROPE // 4)
        q_ref[:, sl] = (qj * scale).astype(BF16)
    ckvf = _dot(h, wdkv_ref[...])
    ckv = _rms(ckvf[:, :kv_lora]) * kvn_ref[...]
    ckv_ref[...] = ckv
    kpe_ref[...] = ckvf[:, kv_lora:]

    @pl.when(pl.program_id(0) < n_prompt_tiles)
    def _():
        ckvp_ref[...] = ckv
        kpep_ref[...] = ckvf[:, kv_lora + MLA_NOPE:kv_lora + MLA_QK_DIM]


def _mla_q(x, rows, norm, modarr, wdq, qan, wuq, qn, wdkv, kvn, cos, sin, layer, n_layers, prev_c, prev_p, bp, sp):
    n, d = x.shape
    tm = TOK_TILE
    ql = wdq.shape[1]
    qw = wuq.shape[1]
    kv_lora = kvn.shape[1]
    full = lambda shape: pl.BlockSpec(shape, lambda i: (0,) * len(shape))
    tok = lambda width: pl.BlockSpec((tm, width), lambda i: (i, 0))
    tab = pl.BlockSpec((tm, LANE), lambda i: (rows.pos_blk(i, tm), 0))
    c_spec, c_sds, c_in, c_in_specs = _cache_out(prev_c, (bp, n_layers, sp, kv_lora), (sp, kv_lora), rows, tm, layer)
    p_spec, p_sds, p_in, p_in_specs = _cache_out(prev_p, (bp, n_layers, sp, MLA_ROPE), (sp, MLA_ROPE), rows, tm, layer)
    n_prev = len(c_in) + len(p_in)
    n_in = 12
    return pl.pallas_call(
        functools.partial(_mla_q_kernel, kv_lora, rows.n_p // tm, n_prev),
        grid=(n // tm,),
        in_specs=[tok(d), full((1, d)), _mod_spec(rows, tm, 0, d), _mod_spec(rows, tm, 1, d),
                  full((d, ql)), full((1, ql)), full((ql, qw)), full((1, LANE)),
                  full((d, kv_lora + LANE)), full((1, kv_lora)), tab, tab] + c_in_specs + p_in_specs,
        out_specs=[tok(qw), tok(kv_lora), tok(LANE), c_spec, p_spec],
        out_shape=[jax.ShapeDtypeStruct((n, qw), BF16), jax.ShapeDtypeStruct((n, kv_lora), F32),
                   jax.ShapeDtypeStruct((n, LANE), F32), c_sds, p_sds],
        input_output_aliases={n_in + a: 3 + a for a in range(n_prev)},
        compiler_params=_params(("arbitrary",)),
        name="mla_q_ckv",
    )(x, norm, modarr, modarr, wdq, qan, wuq, qn, wdkv, kvn, cos, sin, *c_in, *p_in)


def _mla_kv_kernel(ckv_ref, kpe_ref, wuk_ref, wuv_ref, kn_ref, cos_ref, sin_ref, k_ref, v_ref):
    c = ckv_ref[...].astype(BF16)
    kn = _dot(c, wuk_ref[...])
    v_ref[...] = _dot(c, wuv_ref[...]).astype(BF16)
    kpe = kpe_ref[...]
    cos = cos_ref[...]
    sin = sin_ref[...]
    for j in range(kn.shape[1] // LANE):
        sl = slice(j * LANE, (j + 1) * LANE)
        kj = _rope(_head_norm128(kn[:, sl] + kpe, kn_ref[...], MLA_QK_DIM), cos, sin, MLA_ROPE // 4)
        k_ref[:, sl] = kj.astype(BF16)


def _mla_kv(ckv, kpe, wuk, wuv, kn, cos, sin, pos_blk):
    n, kv_lora = ckv.shape
    tm = TOK_TILE
    kw = wuk.shape[1]
    vw = wuv.shape[1]
    full = lambda shape: pl.BlockSpec(shape, lambda i: (0,) * len(shape))
    tok = lambda width: pl.BlockSpec((tm, width), lambda i: (i, 0))
    tab = pl.BlockSpec((tm, LANE), lambda i: (pos_blk(i), 0))
    return pl.pallas_call(
        _mla_kv_kernel,
        grid=(n // tm,),
        in_specs=[tok(kv_lora), tok(LANE), full((kv_lora, kw)), full((kv_lora, vw)), full((1, LANE)), tab, tab],
        out_specs=[tok(kw), tok(vw)],
        out_shape=[jax.ShapeDtypeStruct((n, kw), BF16), jax.ShapeDtypeStruct((n, vw), BF16)],
        compiler_params=_params(("arbitrary",)),
        name="mla_kv_expand",
    )(ckv, kpe, wuk, wuv, kn, cos, sin)


def _sort_pairs(n):
    pairs = []
    p = 1
    while p < n:
        k = p
        while k >= 1:
            for j in range(k % p, n - k, 2 * k):
                for i in range(min(k, n - j - k)):
                    if (i + j) // (2 * p) == (i + j + k) // (2 * p):
                        pairs.append((i + j, i + j + k))
            k //= 2
        p *= 2
    return pairs


def _cmpx(x, i, j):
    a, b = x[i], x[j]
    if b is None:
        return
    if a is None:
        x[i], x[j] = b, None
        return
    x[i], x[j] = jnp.maximum(a, b), jnp.minimum(a, b)


def _top_sorted(x):
    k = PEER_TOPK
    x = list(x)
    for i, j in _sort_pairs(k):
        _cmpx(x, i, j)
    shift = 4
    while shift >= 1:
        y = []
        for i in range(k):
            other = x[k - 1 - i]
            other = None if other is None else pltpu.roll(other, 8 - shift, 0)
            if x[i] is None:
                y.append(other)
            elif other is None:
                y.append(x[i])
            else:
                y.append(jnp.maximum(x[i], other))
        d = k // 2
        while d >= 1:
            for i in range(k):
                if (i & d) == 0:
                    _cmpx(y, i, i + d)
            d //= 2
        x = y
        shift //= 2
    return x


def _pair_candidates(a_ref, b_ref, scale):
    k = PEER_TOPK
    b_lo = b_ref[0:8, :]
    sub = lax.broadcasted_iota(jnp.int32, b_lo.shape, 0)
    row = lambda r: a_ref[r:r + 1, :] * scale
    out = [row(0) * b_lo, row(0) * b_ref[8:16, :]]
    for r in range(1, 8):
        n_r = k // (r + 1)
        c = row(r) * b_lo
        out.append(c if n_r >= 8 else jnp.where(sub < n_r, c, -1.0))
    out.append((a_ref[8:16, :] * scale) * b_ref[0:1, :])
    return out + [None] * (k - len(out))


def _peer_a_kernel(n_heads, x_ref, n_ref, sh_ref, sc_ref, wqt_ref, keys_ref,
                   ht_ref, e1_ref, e0n_ref, tau_ref, qt_sc, e_sc, top_sc):
    k = PEER_TOPK
    h = _modulate(x_ref[...], n_ref[...], sh_ref[0], sc_ref[0])
    ht = h.T.astype(BF16)
    ht_ref[...] = ht
    qt_sc[...] = _dot(wqt_ref[...], ht)

    for hc in range(2 * n_heads):
        qb = qt_sc[hc * PEER_HALF:(hc + 1) * PEER_HALF, :]
        s = _dot(keys_ref[hc], qb.astype(BF16))
        e_sc[hc] = jnp.exp(s - jnp.max(s, axis=0, keepdims=True))

    def per_set(hc, carry):
        e = e_sc[hc]
        top = _top_sorted([e[g * 8:(g + 1) * 8, :] for g in range(PEER_N_KEYS // 8)])
        for r in range(k):
            top_sc[hc, r:r + 1, :] = top[r][0:1, :]
        return carry

    lax.fori_loop(0, 2 * n_heads, per_set, 0)

    def per_head(hh, carry):
        a_ref = top_sc.at[2 * hh]
        b_ref = top_sc.at[2 * hh + 1]
        top = _top_sorted(_pair_candidates(a_ref, b_ref, 1.0))
        z = top[0][0:1, :]
        for r in range(1, k):
            z = z + top[r][0:1, :]
        rz = 1.0 / z
        e0n_ref[hh] = e_sc[2 * hh] * rz
        e1_ref[hh] = e_sc[2 * hh + 1]
        topn = _top_sorted(_pair_candidates(a_ref, b_ref, rz))
        tau_ref[pl.ds(hh, 1), :] = topn[k - 1][0:1, :]
        return carry

    lax.fori_loop(0, n_heads, per_head, 0)


def _peer_a(x, rows, norm, modarr, wqt, keys):
    n, d = x.shape
    t = PEER_A_TILE
    n_sets = keys.shape[0]
    n_heads = n_sets // 2
    full = lambda shape: pl.BlockSpec(shape, lambda i: (0,) * len(shape))
    return pl.pallas_call(
        functools.partial(_peer_a_kernel, n_heads),
        grid=(n // t,),
        in_specs=[pl.BlockSpec((t, d), lambda i: (i, 0)), full((1, d)),
                  _mod_spec(rows, t, 3, d), _mod_spec(rows, t, 4, d),
                  full(wqt.shape), full(keys.shape)],
        out_specs=[pl.BlockSpec((d, t), lambda i: (0, i)),
                   pl.BlockSpec((n_heads, PEER_N_KEYS, t), lambda i: (0, 0, i)),
                   pl.BlockSpec((n_heads, PEER_N_KEYS, t), lambda i: (0, 0, i)),
                   pl.BlockSpec((n_heads, t), lambda i: (0, i))],
        out_shape=[jax.ShapeDtypeStruct((d, n), BF16),
                   jax.ShapeDtypeStruct((n_heads, PEER_N_KEYS, n), F32),
                   jax.ShapeDtypeStruct((n_heads, PEER_N_KEYS, n), F32),
                   jax.ShapeDtypeStruct((n_heads, n), F32)],
        scratch_shapes=[pltpu.VMEM((n_sets * PEER_HALF, t), F32),
                        pltpu.VMEM((n_sets, PEER_N_KEYS, t), F32),
                        pltpu.VMEM((n_sets, PEER_TOPK, t), F32)],
        compiler_params=_params(("arbitrary",)),
        name="peer_select",
    )(x, norm, modarr, modarr, wqt, keys)


def _peer_b_kernel(n_heads, n_chunks, ht_ref, u_ref, vt_ref, e1_ref, e0n_ref, tau_ref, x_ref, g_ref, y_ref,
                   acc_sc, *chunk_sc):
    at_sc, w_sc = chunk_sc[:n_chunks], chunk_sc[n_chunks:]
    e = pl.program_id(1)

    @pl.when(e == 0)
    def _():
        acc_sc[...] = jnp.zeros_like(acc_sc)

    ht = ht_ref[...]
    t = ht.shape[1]
    ch = PEER_B_CHUNK
    for c in range(n_chunks):
        at_sc[c][...] = _dot(u_ref[c * ch:(c + 1) * ch, :], ht)
    for c in range(n_chunks):
        for r0 in range(0, ch, 64):
            ii, j0 = divmod(c * ch + r0, PEER_N_KEYS)
            for l0 in range(0, t, LANE):
                ls = slice(l0, l0 + LANE)
                gate = None
                for hh in range(n_heads):
                    p = e1_ref[hh, j0:j0 + 64, ls] * e0n_ref[hh, ii:ii + 1, ls]
                    g = jnp.where(p >= tau_ref[hh:hh + 1, ls], p, 0.0)
                    gate = g if gate is None else gate + g
                w_sc[c][r0:r0 + 64, ls] = (gate * jax.nn.gelu(at_sc[c][r0:r0 + 64, ls])).astype(BF16)
        acc_sc[...] += _dot(vt_ref[:, c * ch:(c + 1) * ch], w_sc[c][...])

    @pl.when(e == pl.num_programs(1) - 1)
    def _():
        y_ref[...] = x_ref[...] + g_ref[0] * acc_sc[...].T


def _peer_b(ht, u, vt, layer, e1, e0n, tau, x, rows, modarr):
    n, d = x.shape
    t = PEER_B_TILE
    te = PEER_E_TILE
    n_exp = u.shape[1]
    n_heads = e1.shape[0]
    ti = te // PEER_N_KEYS
    n_chunks = te // PEER_B_CHUNK
    return pl.pallas_call(
        functools.partial(_peer_b_kernel, n_heads, n_chunks),
        grid=(n // t, n_exp // te),
        in_specs=[pl.BlockSpec((d, t), lambda i, e: (0, i)),
                  pl.BlockSpec((None, te, d), lambda i, e: (layer, e, 0)),
                  pl.BlockSpec((None, None, d, te), lambda i, e: (layer, e, 0, 0)),
                  pl.BlockSpec((n_heads, PEER_N_KEYS, t), lambda i, e: (0, 0, i)),
                  pl.BlockSpec((n_heads, ti, t), lambda i, e: (0, e, i)),
                  pl.BlockSpec((n_heads, t), lambda i, e: (0, i)),
                  pl.BlockSpec((t, d), lambda i, e: (i, 0)),
                  pl.BlockSpec((1, 1, d), lambda i, e: (rows.mod_row(i, t) * 6 + 5, 0, 0))],
        out_specs=pl.BlockSpec((t, d), lambda i, e: (i, 0)),
        out_shape=jax.ShapeDtypeStruct((n, d), F32),
        scratch_shapes=([pltpu.VMEM((d, t), F32)] + [pltpu.VMEM((PEER_B_CHUNK, t), F32)] * n_chunks
                        + [pltpu.VMEM((PEER_B_CHUNK, t), BF16)] * n_chunks),
        compiler_params=_params(("arbitrary", "arbitrary")),
        name="peer_experts",
    )(ht, u, vt, e1, e0n, tau, x, modarr)


def _pad_heads(w, n_heads, width):
    k = w.shape[0]
    w = w.reshape(k, n_heads, width)
    return jnp.pad(w, ((0, 0), (0, 0), (0, LANE - width))).reshape(k, n_heads * LANE)


def _pad_lanes(g, lo=0):
    g = g.reshape(1, -1).astype(F32)
    return jnp.pad(g, ((0, 0), (lo, LANE - lo - g.shape[1])))


def kernel(x_prompt, x_sample, cache_da_k, cache_da_v, cache_mla_ckv, cache_mla_kpe, c, c_ctx, norm1, norm2, ada_w, ada_b, da_wq, da_wk, da_wv, da_wo, da_q_norm, da_k_norm, da_lq1, da_lk1, da_lq2, da_lk2, da_subln, mla_wdq, mla_qa_norm, mla_wuq, mla_wdkv, mla_kv_norm, mla_wukv, mla_q_norm, mla_k_norm, mla_wo, peer_wq, peer_keys, peer_u, peer_v):
    bp, sp, d = x_prompt.shape
    bs, ss, _ = x_sample.shape
    depth = norm1.shape[0]
    past = cache_da_k.shape[2]
    n_p = bp * sp
    rows = _Rows(n_p, bs, ss)
    tm = TOK_TILE
    assert sp % tm == 0 and ss % tm == 0 and n_p % ss == 0 and past == sp
    assert n_p % PEER_B_TILE == 0 and ss % PEER_B_TILE == 0

    da_heads = da_wv.shape[2] // DA_V_DIM
    mla_heads = mla_wo.shape[1] // MLA_V_DIM
    kv_lora = mla_kv_norm.shape[1]
    peer_heads = peer_keys.shape[1]

    x = jnp.concatenate([x_prompt.reshape(n_p, d), x_sample.reshape(bs * ss, d)], axis=0)

    n_cond = 1 + bs
    r_pad = -(-n_cond // 8) * 8
    cvecs = jnp.concatenate([c_ctx[None, :], c, jnp.zeros((r_pad - n_cond, d), F32)], axis=0)
    mods = _modulation(cvecs, ada_w, ada_b).reshape(depth, r_pad * 6, 1, d)

    da_cos, da_sin = _rope_tables(ss, tm, 0, DA_HEAD_DIM, DA_HEAD_DIM)
    mla_cos, mla_sin = _rope_tables(ss, tm, MLA_NOPE, MLA_ROPE, LANE)
    ident = ss // tm

    assert tm == sp
    n_da = (depth + 1) // 2
    n_mla = depth // 2
    da_k = da_v = mla_c = mla_p = None
    u_all = peer_u.astype(BF16)
    n_et = peer_v.shape[1] // PEER_E_TILE
    vt_all = jnp.swapaxes(peer_v.reshape(depth, n_et, PEER_E_TILE, d), 2, 3).astype(BF16)
    for l in range(depth):
        modarr = mods[l]
        j = l // 2
        if l % 2 == 0:
            lam_init = 0.8 - 0.6 * math.exp(-0.3 * l)
            q, kb, vb, da_k, da_v = _da_qkv(
                x, rows, norm1[l][None, :], modarr,
                da_wq[j].astype(BF16), da_wk[j].astype(BF16), da_wv[j].astype(BF16),
                jnp.tile(da_q_norm[j][None, :], (1, 2)), jnp.tile(da_k_norm[j][None, :], (1, 2)), da_cos, da_sin,
                j, n_da, da_k, da_v, bp, sp)
            lam_vecs = jnp.stack([da_lq1[j], da_lk1[j], da_lq2[j], da_lk2[j]]).astype(F32)
            extra = [lam_vecs, da_subln[j][None, :]]
            extra_specs = [pl.BlockSpec(lam_vecs.shape, lambda b, hp, qi: (0, 0)),
                           pl.BlockSpec((1, DA_V_DIM), lambda b, hp, qi: (0, 0))]
            ck = cache_da_k[:, j].reshape(bs * past, 2 * da_heads * DA_HEAD_DIM)
            cv = cache_da_v[:, j].reshape(bs * past, da_heads * DA_V_DIM)
            o_p = _attention(functools.partial(_da_attn_kernel, lam_init, 1), da_heads, da_heads, LANE, q, sp, 0, bp,
                             [(kb, vb, sp, 0)], extra, extra_specs, "da_attn_prompt")
            o_s = _attention(functools.partial(_da_attn_kernel, lam_init, 2), da_heads, 1, LANE, q, ss, n_p, bs,
                             [(kb, vb, ss, n_p), (ck, cv, past, 0)], extra, extra_specs, "da_attn_latent")
            wo = da_wo[j].astype(BF16)
        else:
            wuq = _pad_heads(mla_wuq[j], mla_heads, MLA_QK_DIM).astype(BF16)
            wdkv = jnp.concatenate([
                mla_wdkv[j][:, :kv_lora],
                jnp.pad(mla_wdkv[j][:, kv_lora:], ((0, 0), (MLA_NOPE, LANE - MLA_QK_DIM)))], axis=1).astype(BF16)
            wukv = mla_wukv[j].reshape(kv_lora, mla_heads, MLA_NOPE + MLA_V_DIM)
            wuk = jnp.pad(wukv[:, :, :MLA_NOPE], ((0, 0), (0, 0), (0, LANE - MLA_NOPE)))
            wuk = wuk.reshape(kv_lora, mla_heads * LANE).astype(BF16)
            wuv = wukv[:, :, MLA_NOPE:].reshape(kv_lora, mla_heads * MLA_V_DIM).astype(BF16)
            q, ckv, kpe, mla_c, mla_p = _mla_q(
                x, rows, norm1[l][None, :], modarr, mla_wdq[j].astype(BF16), mla_qa_norm[j][None, :], wuq,
                _pad_lanes(mla_q_norm[j]), wdkv, mla_kv_norm[j][None, :], mla_cos, mla_sin,
                j, n_mla, mla_c, mla_p, bp, sp)
            kn = _pad_lanes(mla_k_norm[j])
            k, v = _mla_kv(ckv, kpe, wuk, wuv, kn, mla_cos, mla_sin, lambda i: rows.pos_blk(i, tm))
            c_ckv = cache_mla_ckv[:, j].reshape(bs * past, kv_lora)
            c_kpe = jnp.pad(cache_mla_kpe[:, j].reshape(bs * past, MLA_ROPE), ((0, 0), (MLA_NOPE, LANE - MLA_QK_DIM)))
            ck, cv = _mla_kv(c_ckv, c_kpe, wuk, wuv, kn, mla_cos, mla_sin, lambda i: ident)
            o_p = _attention(functools.partial(_mla_attn_kernel, 1), mla_heads // 2, mla_heads // 2, 2 * LANE, q, sp, 0, bp,
                             [(k, v, sp, 0)], [], [], "mla_attn_prompt")
            o_s = _attention(functools.partial(_mla_attn_kernel, 2), mla_heads // 2, 1, 2 * LANE, q, ss, n_p, bs,
                             [(k, v, ss, n_p), (ck, cv, past, 0)], [], [], "mla_attn_latent")
            wo = mla_wo[j].astype(BF16)
        x = _out_proj(o_p, o_s, wo, x, rows, modarr, 2)

        wqt = peer_wq[l].T.astype(BF16)
        keys = peer_keys[l].reshape(2 * peer_heads, PEER_N_KEYS, PEER_HALF).astype(BF16)
        ht, e1, e0n, tau = _peer_a(x, rows, norm2[l][None, :], modarr, wqt, keys)
        x = _peer_b(ht, u_all, vt_all, l, e1, e0n, tau, x, rows, modarr)

    return (x[:n_p].reshape(bp, sp, d), x[n_p:].reshape(bs, ss, d),
            da_k.reshape(bp, n_da, sp, 2 * da_heads, DA_HEAD_DIM), da_v.reshape(bp, n_da, sp, da_heads, DA_V_DIM),
            mla_c, mla_p)
```

```python
import functools
import math

import jax
import jax.numpy as jnp
from jax import lax
from jax.experimental import pallas as pl
from jax.experimental.pallas import tpu as pltpu

GRID_W = 64
DA_HEAD_DIM = 64
DA_V_DIM = 2 * DA_HEAD_DIM
MLA_NOPE = 64
MLA_ROPE = 32
MLA_QK_DIM = MLA_NOPE + MLA_ROPE
MLA_V_DIM = 64
PEER_N_KEYS = 128
PEER_HALF = 128
PEER_TOPK = 16
ROPE_BASE = 10000.0
EPS = 1e-6

LANE = 128
VMEM_LIMIT = 56 * 1024 * 1024
TOK_TILE = 256
Q_TILE = 1024
KEY_CHUNK = 1024
LOG2E = math.log2(math.e)
PEER_A_TILE = 256
PEER_B_TILE = 512
PEER_E_TILE = 2048
PEER_B_CHUNK = 256
PEER_OUT_ROWS = 512
GELU_C0 = math.sqrt(2.0 / math.pi)
GELU_C1 = GELU_C0 * 0.044715

BF16 = jnp.bfloat16
F32 = jnp.float32


def _params(sem, flags=None):
    return pltpu.CompilerParams(dimension_semantics=sem, vmem_limit_bytes=VMEM_LIMIT, flags=flags)


def _dot(a, b):
    return jnp.dot(a, b, preferred_element_type=F32)


def _dot_nt(a, b):
    return lax.dot_general(a, b, (((1,), (1,)), ((), ())), preferred_element_type=F32)


def _rms(x):
    return x * lax.rsqrt(jnp.mean(x * x, axis=-1, keepdims=True) + EPS)


def _modulate(x, g, shift, scale):
    return _rms(x) * g * (1.0 + scale) + shift


def _rope(x, cos, sin, half):
    lane = lax.broadcasted_iota(jnp.int32, x.shape, 1)
    first = (lane & (2 * half - 1)) < half
    partner = jnp.where(first, pltpu.roll(x, LANE - half, 1), pltpu.roll(x, half, 1))
    return x * cos + partner * sin


class _Rows:
    def __init__(self, n_p, n_b, s_lat):
        self.n_p, self.n_b, self.s_lat = n_p, n_b, s_lat
        self.n = n_p + n_b * s_lat

    def mod_row(self, i, tm):
        np_t = self.n_p // tm
        return jnp.where(i < np_t, 0, 1 + (i - np_t) // (self.s_lat // tm))

    def pos_blk(self, i, tm):
        np_t = self.n_p // tm
        return jnp.where(i < np_t, self.s_lat // tm, (i - np_t) % (self.s_lat // tm))


def _mod_spec(rows, tm, which, d):
    return pl.BlockSpec((1, 1, d), lambda i: (rows.mod_row(i, tm) * 6 + which, 0, 0))


def _rope_tables(s_lat, tm, rot_lo, rot_w, period):
    hd = rot_w // 2
    half = hd // 2
    lane = jnp.arange(LANE)
    d = (lane % period) - rot_lo
    rot = (d >= 0) & (d < rot_w)
    d = jnp.clip(d, 0, rot_w - 1)
    use_row = d < hd
    dd = d % hd
    i = dd % half
    first = dd < half
    inv = 1.0 / (ROPE_BASE ** ((2 * i).astype(F32) / hd))
    t = jnp.arange(s_lat)
    row = (t // GRID_W).astype(F32)
    col = (t % GRID_W).astype(F32)
    pos = jnp.where(use_row[None, :], row[:, None], col[:, None])
    ang = pos * inv[None, :]
    cos = jnp.where(rot[None, :], jnp.cos(ang), 1.0)
    sin = jnp.where(rot[None, :], jnp.where(first[None, :], -jnp.sin(ang), jnp.sin(ang)), 0.0)
    cos = jnp.concatenate([cos, jnp.ones((tm, LANE), F32)], axis=0)
    sin = jnp.concatenate([sin, jnp.zeros((tm, LANE), F32)], axis=0)
    return cos.astype(F32), sin.astype(F32)


def _cache_out(prev, shape, blk_tail, rows, tm, layer):
    np_t = rows.n_p // tm
    spec = pl.BlockSpec((None, None) + blk_tail, lambda i: (jnp.minimum(i, np_t - 1), layer, 0, 0))
    sds = jax.ShapeDtypeStruct(shape, F32)
    if prev is None:
        prev = jnp.zeros(shape, F32)
    return spec, sds, [prev], [pl.BlockSpec(memory_space=pl.ANY)]


def _mod_kernel(c_ref, w_ref, b_ref, o_ref):
    c = c_ref[...]
    s = c * jax.nn.sigmoid(c)
    o_ref[...] = _dot(s.astype(BF16), w_ref[...].astype(BF16)) + b_ref[...]


def _modulation(cvecs, ada_w, ada_b):
    depth, d, d6 = ada_w.shape
    r = cvecs.shape[0]
    tn = d
    return pl.pallas_call(
        _mod_kernel,
        grid=(depth, d6 // tn),
        in_specs=[
            pl.BlockSpec((r, d), lambda l, j: (0, 0)),
            pl.BlockSpec((None, d, tn), lambda l, j: (l, 0, j)),
            pl.BlockSpec((None, 1, tn), lambda l, j: (l, 0, j)),
        ],
        out_specs=pl.BlockSpec((None, r, tn), lambda l, j: (l, 0, j)),
        out_shape=jax.ShapeDtypeStruct((depth, r, d6), F32),
        compiler_params=_params(("arbitrary", "arbitrary")),
        name="adaln_modulation",
    )(cvecs, ada_w, ada_b.reshape(depth, 1, d6))


def _da_qkv_kernel(n_prompt_tiles, n_prev, x_ref, n_ref, sh_ref, sc_ref, wq_ref, wk_ref, wv_ref, qn_ref, kn_ref,
                   cos_ref, sin_ref, *refs):
    q_ref, kb_ref, vb_ref, kf_ref, vf_ref = refs[n_prev:]
    h = _modulate(x_ref[...], n_ref[...], sh_ref[0], sc_ref[0]).astype(BF16)
    q = _dot(h, wq_ref[...])
    k = _dot(h, wk_ref[...])
    v = _dot(h, wv_ref[...])
    vb_ref[...] = v.astype(BF16)
    cos = cos_ref[...]
    sin = sin_ref[...]
    lane = lax.broadcasted_iota(jnp.int32, (x_ref.shape[0], LANE), 1)
    lo = lane < DA_HEAD_DIM

    def head_norm(t, g):
        sq = t * t
        s_lo = jnp.sum(jnp.where(lo, sq, 0.0), axis=-1, keepdims=True)
        s_hi = jnp.sum(jnp.where(lo, 0.0, sq), axis=-1, keepdims=True)
        ms = jnp.where(lo, s_lo, s_hi) * (1.0 / DA_HEAD_DIM)
        return t * lax.rsqrt(ms + EPS) * g

    scale = DA_HEAD_DIM ** -0.5 * LOG2E
    ks = []
    for j in range(q.shape[1] // LANE):
        sl = slice(j * LANE, (j + 1) * LANE)
        qj = _rope(head_norm(q[:, sl], qn_ref[...]), cos, sin, DA_HEAD_DIM // 4)
        kj = _rope(head_norm(k[:, sl], kn_ref[...]), cos, sin, DA_HEAD_DIM // 4)
        q_ref[:, sl] = (qj * scale).astype(BF16)
        kb_ref[:, sl] = kj.astype(BF16)
        ks.append(kj)

    @pl.when(pl.program_id(0) < n_prompt_tiles)
    def _():
        vf_ref[...] = v
        for j, kj in enumerate(ks):
            kf_ref[:, j * LANE:(j + 1) * LANE] = kj


def _da_qkv(x, rows, norm, modarr, wq, wk, wv, qn, kn, cos, sin, layer, n_layers, prev_k, prev_v, bp, sp):
    n, d = x.shape
    tm = TOK_TILE
    w = wq.shape[1]
    full = lambda shape: pl.BlockSpec(shape, lambda i: (0,) * len(shape))
    tok = lambda width: pl.BlockSpec((tm, width), lambda i: (i, 0))
    tab = pl.BlockSpec((tm, LANE), lambda i: (rows.pos_blk(i, tm), 0))
    k_spec, k_sds, k_in, k_in_specs = _cache_out(prev_k, (bp, n_layers, sp, w), (sp, w), rows, tm, layer)
    v_spec, v_sds, v_in, v_in_specs = _cache_out(prev_v, (bp, n_layers, sp, w), (sp, w), rows, tm, layer)
    n_prev = len(k_in) + len(v_in)
    n_in = 11
    return pl.pallas_call(
        functools.partial(_da_qkv_kernel, rows.n_p // tm, n_prev),
        grid=(n // tm,),
        in_specs=[tok(d), full((1, d)), _mod_spec(rows, tm, 0, d), _mod_spec(rows, tm, 1, d),
                  full((d, w)), full((d, w)), full((d, w)), full((1, LANE)), full((1, LANE)), tab, tab]
        + k_in_specs + v_in_specs,
        out_specs=[tok(w)] * 3 + [k_spec, v_spec],
        out_shape=[jax.ShapeDtypeStruct((n, w), BF16)] * 3 + [k_sds, v_sds],
        input_output_aliases={n_in + a: 3 + a for a in range(n_prev)},
        compiler_params=_params(("arbitrary",)),
        name="da_qkv",
    )(x, norm, modarr, modarr, wq, wk, wv, qn, kn, cos, sin, *k_in, *v_in)


def _attend(qs, ksls, vsl, kv_refs):
    n = len(qs)
    m, acc = [None] * n, [None] * n
    for k_ref, v_ref in kv_refs:
        sk = k_ref.shape[0]
        step = min(KEY_CHUNK, sk)
        for c0 in range(0, sk, step):
            v = v_ref[c0:c0 + step, vsl].astype(BF16)
            v1 = jnp.concatenate([v, jnp.ones_like(v)], axis=1)
            for a in range(n):
                s = _dot_nt(qs[a], k_ref[c0:c0 + step, ksls[a]].astype(BF16))
                mx = s.max(axis=-1, keepdims=True)
                if m[a] is None:
                    m[a] = mx
                    acc[a] = _dot(jnp.exp2((s - mx).astype(BF16)), v1)
                else:
                    m_new = jnp.maximum(m[a], mx)
                    alpha = jnp.exp2(m[a] - m_new)
                    acc[a] = alpha * acc[a] + _dot(jnp.exp2((s - m_new).astype(BF16)), v1)
                    m[a] = m_new
    return [acc[a][:, :LANE] / acc[a][:, LANE:] for a in range(n)]


def _da_attn_kernel(lam_init, n_chunks, group, lam_ref, subln_ref, q_ref, *refs):
    kv = [(refs[2 * c], refs[2 * c + 1]) for c in range(n_chunks)]
    o_ref = refs[2 * n_chunks]
    lv = lam_ref[...]
    lam = (jnp.exp(jnp.sum(lv[0:1] * lv[1:2], axis=-1, keepdims=True))
           - jnp.exp(jnp.sum(lv[2:3] * lv[3:4], axis=-1, keepdims=True)) + lam_init)
    for g in range(group):
        sl = slice(g * LANE, (g + 1) * LANE)
        q = q_ref[:, sl]
        lane = lax.broadcasted_iota(jnp.int32, q.shape, 1)
        zero = jnp.zeros_like(q)
        qs = [jnp.where(lane < DA_HEAD_DIM, q, zero), jnp.where(lane < DA_HEAD_DIM, zero, q)]
        o1, o2 = _attend(qs, [sl] * 2, sl, kv)
        o = o1 - lam * o2
        o = _rms(o) * subln_ref[...] * (1.0 - lam_init)
        o_ref[:, sl] = o.astype(o_ref.dtype)


def _mla_attn_kernel(n_chunks, group, q_ref, *refs):
    kv = [(refs[2 * c], refs[2 * c + 1]) for c in range(n_chunks)]
    o_ref = refs[2 * n_chunks]
    for g in range(group):
        sls = [slice((2 * g + half) * LANE, (2 * g + half + 1) * LANE) for half in range(2)]
        vsl = slice(g * LANE, (g + 1) * LANE)
        outs = _attend([q_ref[:, sl] for sl in sls], sls, vsl, kv)
        lane = lax.broadcasted_iota(jnp.int32, outs[0].shape, 1)
        o_ref[:, vsl] = jnp.where(lane < MLA_V_DIM, outs[0], outs[1]).astype(o_ref.dtype)


def _attention(kernel, n_pairs, group, qw, q, seq, row_off, n_b, chunks, extra_in, extra_specs, name):
    tq = min(Q_TILE, seq)
    nq = seq // tq
    q_spec = pl.BlockSpec((tq, qw * group), lambda b, hp, qi: (row_off // tq + b * nq + qi, hp))
    in_specs = list(extra_specs) + [q_spec]
    args = list(extra_in) + [q]
    for k, v, sk, off in chunks:
        kv_map = functools.partial(lambda b, hp, qi, o, s: (o // s + b, hp), o=off, s=sk)
        in_specs.append(pl.BlockSpec((sk, qw * group), kv_map))
        in_specs.append(pl.BlockSpec((sk, LANE * group), kv_map))
        args += [k, v]
    return pl.pallas_call(
        functools.partial(kernel, group),
        grid=(n_b, n_pairs // group, nq),
        in_specs=in_specs,
        out_specs=pl.BlockSpec((tq, LANE * group), lambda b, hp, qi: (b * nq + qi, hp)),
        out_shape=jax.ShapeDtypeStruct((n_b * seq, n_pairs * LANE), BF16),
        compiler_params=_params(("arbitrary", "arbitrary", "arbitrary")),
        name=name,
    )(*args)


def _out_proj_kernel(n_prompt_tiles, op_ref, os_ref, w_ref, x_ref, g_ref, y_ref):
    o = jnp.where(pl.program_id(0) < n_prompt_tiles, op_ref[...], os_ref[...])
    y_ref[...] = x_ref[...] + g_ref[0] * _dot(o, w_ref[...])


def _out_proj(o_p, o_s, w, x, rows, modarr, which):
    n, d = x.shape
    tm = TOK_TILE
    k = o_p.shape[1]
    np_t = rows.n_p // tm
    return pl.pallas_call(
        functools.partial(_out_proj_kernel, np_t),
        grid=(n // tm,),
        in_specs=[pl.BlockSpec((tm, k), lambda i: (jnp.minimum(i, np_t - 1), 0)),
                  pl.BlockSpec((tm, k), lambda i: (jnp.maximum(i - np_t, 0), 0)),
                  pl.BlockSpec((k, d), lambda i: (0, 0)),
                  pl.BlockSpec((tm, d), lambda i: (i, 0)), _mod_spec(rows, tm, which, d)],
        out_specs=pl.BlockSpec((tm, d), lambda i: (i, 0)),
        out_shape=jax.ShapeDtypeStruct((n, d), F32),
        compiler_params=_params(("arbitrary",)),
        name="out_proj",
    )(o_p, o_s, w, x, modarr)


def _head_norm128(t, g, width):
    ms = jnp.sum(t * t, axis=-1, keepdims=True) * (1.0 / width)
    return t * lax.rsqrt(ms + EPS) * g


def _mla_q_kernel(kv_lora, n_prompt_tiles, n_prev, x_ref, n_ref, sh_ref, sc_ref, wdq_ref, qan_ref, wuq_ref, qn_ref,
                  wdkv_ref, kvn_ref, cos_ref, sin_ref, *refs):
    q_ref, ckv_ref, kpe_ref, ckvp_ref, kpep_ref = refs[n_prev:]
    h = _modulate(x_ref[...], n_ref[...], sh_ref[0], sc_ref[0]).astype(BF16)
    cq = (_rms(_dot(h, wdq_ref[...])) * qan_ref[...]).astype(BF16)
    q = _dot(cq, wuq_ref[...])
    cos = cos_ref[...]
    sin = sin_ref[...]
    scale = MLA_QK_DIM ** -0.5 * LOG2E
    for j in range(q.shape[1] // LANE):
        sl = slice(j * LANE, (j + 1) * LANE)
        qj = _rope(_head_norm128(q[:, sl], qn_ref[...], MLA_QK_DIM), cos, sin, MLA_ROPE // 4)
        q_ref[:, sl] = (qj * scale).astype(BF16)
    ckvf = _dot(h, wdkv_ref[...])
    ckv = _rms(ckvf[:, :kv_lora]) * kvn_ref[...]
    ckv_ref[...] = ckv
    kpe_ref[...] = ckvf[:, kv_lora:]

    @pl.when(pl.program_id(0) < n_prompt_tiles)
    def _():
        ckvp_ref[...] = ckv
        kpep_ref[...] = ckvf[:, kv_lora + MLA_NOPE:kv_lora + MLA_QK_DIM]


def _mla_q(x, rows, norm, modarr, wdq, qan, wuq, qn, wdkv, kvn, cos, sin, layer, n_layers, prev_c, prev_p, bp, sp):
    n, d = x.shape
    tm = TOK_TILE
    ql = wdq.shape[1]
    qw = wuq.shape[1]
    kv_lora = kvn.shape[1]
    full = lambda shape: pl.BlockSpec(shape, lambda i: (0,) * len(shape))
    tok = lambda width: pl.BlockSpec((tm, width), lambda i: (i, 0))
    tab = pl.BlockSpec((tm, LANE), lambda i: (rows.pos_blk(i, tm), 0))
    c_spec, c_sds, c_in, c_in_specs = _cache_out(prev_c, (bp, n_layers, sp, kv_lora), (sp, kv_lora), rows, tm, layer)
    p_spec, p_sds, p_in, p_in_specs = _cache_out(prev_p, (bp, n_layers, sp, MLA_ROPE), (sp, MLA_ROPE), rows, tm, layer)
    n_prev = len(c_in) + len(p_in)
    n_in = 12
    return pl.pallas_call(
        functools.partial(_mla_q_kernel, kv_lora, rows.n_p // tm, n_prev),
        grid=(n // tm,),
        in_specs=[tok(d), full((1, d)), _mod_spec(rows, tm, 0, d), _mod_spec(rows, tm, 1, d),
                  full((d, ql)), full((1, ql)), full((ql, qw)), full((1, LANE)),
                  full((d, kv_lora + LANE)), full((1, kv_lora)), tab, tab] + c_in_specs + p_in_specs,
        out_specs=[tok(qw), tok(kv_lora), tok(LANE), c_spec, p_spec],
        out_shape=[jax.ShapeDtypeStruct((n, qw), BF16), jax.ShapeDtypeStruct((n, kv_lora), F32),
                   jax.ShapeDtypeStruct((n, LANE), F32), c_sds, p_sds],
        input_output_aliases={n_in + a: 3 + a for a in range(n_prev)},
        compiler_params=_params(("arbitrary",)),
        name="mla_q_ckv",
    )(x, norm, modarr, modarr, wdq, qan, wuq, qn, wdkv, kvn, cos, sin, *c_in, *p_in)


def _mla_kv_kernel(ckv_ref, kpe_ref, wuk_ref, wuv_ref, kn_ref, cos_ref, sin_ref, k_ref, v_ref):
    c = ckv_ref[...].astype(BF16)
    kn = _dot(c, wuk_ref[...])
    v_ref[...] = _dot(c, wuv_ref[...]).astype(BF16)
    kpe = kpe_ref[...]
    cos = cos_ref[...]
    sin = sin_ref[...]
    for j in range(kn.shape[1] // LANE):
        sl = slice(j * LANE, (j + 1) * LANE)
        kj = _rope(_head_norm128(kn[:, sl] + kpe, kn_ref[...], MLA_QK_DIM), cos, sin, MLA_ROPE // 4)
        k_ref[:, sl] = kj.astype(BF16)


def _mla_kv(ckv, kpe, wuk, wuv, kn, cos, sin, pos_blk):
    n, kv_lora = ckv.shape
    tm = TOK_TILE
    kw = wuk.shape[1]
    vw = wuv.shape[1]
    full = lambda shape: pl.BlockSpec(shape, lambda i: (0,) * len(shape))
    tok = lambda width: pl.BlockSpec((tm, width), lambda i: (i, 0))
    tab = pl.BlockSpec((tm, LANE), lambda i: (pos_blk(i), 0))
    return pl.pallas_call(
        _mla_kv_kernel,
        grid=(n // tm,),
        in_specs=[tok(kv_lora), tok(LANE), full((kv_lora, kw)), full((kv_lora, vw)), full((1, LANE)), tab, tab],
        out_specs=[tok(kw), tok(vw)],
        out_shape=[jax.ShapeDtypeStruct((n, kw), BF16), jax.ShapeDtypeStruct((n, vw), BF16)],
        compiler_params=_params(("arbitrary",)),
        name="mla_kv_expand",
    )(ckv, kpe, wuk, wuv, kn, cos, sin)


def _sort_pairs(n):
    pairs = []
    p = 1
    while p < n:
        k = p
        while k >= 1:
            for j in range(k % p, n - k, 2 * k):
                for i in range(min(k, n - j - k)):
                    if (i + j) // (2 * p) == (i + j + k) // (2 * p):
                        pairs.append((i + j, i + j + k))
            k //= 2
        p *= 2
    return pairs


def _cmpx(x, i, j):
    a, b = x[i], x[j]
    if b is None:
        return
    if a is None:
        x[i], x[j] = b, None
        return
    x[i], x[j] = jnp.maximum(a, b), jnp.minimum(a, b)


def _top_sorted(x):
    k = PEER_TOPK
    x = list(x)
    for i, j in _sort_pairs(k):
        _cmpx(x, i, j)
    shift = 4
    while shift >= 1:
        y = []
        for i in range(k):
            other = x[k - 1 - i]
            other = None if other is None else pltpu.roll(other, 8 - shift, 0)
            if x[i] is None:
                y.append(other)
            elif other is None:
                y.append(x[i])
            else:
                y.append(jnp.maximum(x[i], other))
        d = k // 2
        while d >= 1:
            for i in range(k):
                if (i & d) == 0:
                    _cmpx(y, i, i + d)
            d //= 2
        x = y
        shift //= 2
    return x


def _pair_candidates(a_ref, b_ref, scale):
    k = PEER_TOPK
    b_lo = b_ref[0:8, :]
    sub = lax.broadcasted_iota(jnp.int32, b_lo.shape, 0)
    row = lambda r: a_ref[r:r + 1, :] * scale
    out = [row(0) * b_lo, row(0) * b_ref[8:16, :]]
    for r in range(1, 8):
        n_r = k // (r + 1)
        c = row(r) * b_lo
        out.append(c if n_r >= 8 else jnp.where(sub < n_r, c, -1.0))
    out.append((a_ref[8:16, :] * scale) * b_ref[0:1, :])
    return out + [None] * (k - len(out))


def _peer_a_kernel(n_heads, x_ref, n_ref, sh_ref, sc_ref, wqt_ref, keys_ref,
                   ht_ref, e1_ref, e0n_ref, tau_ref, qt_sc, e_sc, top_sc):
    k = PEER_TOPK
    h = _modulate(x_ref[...], n_ref[...], sh_ref[0], sc_ref[0])
    ht = h.T.astype(BF16)
    ht_ref[...] = ht
    qt_sc[...] = _dot(wqt_ref[...], ht)

    for hc in range(2 * n_heads):
        qb = qt_sc[hc * PEER_HALF:(hc + 1) * PEER_HALF, :]
        s = _dot(keys_ref[hc], qb.astype(BF16))
        e_sc[hc] = jnp.exp(s - jnp.max(s, axis=0, keepdims=True))

    def per_set(hc, carry):
        e = e_sc[hc]
        top = _top_sorted([e[g * 8:(g + 1) * 8, :] for g in range(PEER_N_KEYS // 8)])
        for r in range(k):
            top_sc[hc, r:r + 1, :] = top[r][0:1, :]
        return carry

    lax.fori_loop(0, 2 * n_heads, per_set, 0)

    def per_head(hh, carry):
        a_ref = top_sc.at[2 * hh]
        b_ref = top_sc.at[2 * hh + 1]
        cands = _pair_candidates(a_ref, b_ref, 1.0)
        top = _top_sorted(cands)
        z = top[0][0:1, :]
        for r in range(1, k):
            z = z + top[r][0:1, :]
        rz = 0.5 / z
        e0n_ref[hh] = e_sc[2 * hh] * rz
        e1_ref[hh] = e_sc[2 * hh + 1]
        kth = top[k - 1][0:1, :]
        taun = None
        for cu, cn in zip(cands, _pair_candidates(a_ref, b_ref, rz)):
            if cu is not None:
                v = jnp.where(cu == kth, cn, jnp.inf)
                taun = v if taun is None else jnp.minimum(taun, v)
        tau_ref[pl.ds(hh, 1), :] = jnp.min(taun, axis=0, keepdims=True)
        return carry

    lax.fori_loop(0, n_heads, per_head, 0)


def _peer_a(x, rows, norm, modarr, wqt, keys):
    n, d = x.shape
    t = PEER_A_TILE
    n_sets = keys.shape[0]
    n_heads = n_sets // 2
    full = lambda shape: pl.BlockSpec(shape, lambda i: (0,) * len(shape))
    return pl.pallas_call(
        functools.partial(_peer_a_kernel, n_heads),
        grid=(n // t,),
        in_specs=[pl.BlockSpec((t, d), lambda i: (i, 0)), full((1, d)),
                  _mod_spec(rows, t, 3, d), _mod_spec(rows, t, 4, d),
                  full(wqt.shape), full(keys.shape)],
        out_specs=[pl.BlockSpec((d, t), lambda i: (0, i)),
                   pl.BlockSpec((n_heads, PEER_N_KEYS, t), lambda i: (0, 0, i)),
                   pl.BlockSpec((n_heads, PEER_N_KEYS, t), lambda i: (0, 0, i)),
                   pl.BlockSpec((n_heads, t), lambda i: (0, i))],
        out_shape=[jax.ShapeDtypeStruct((d, n), BF16),
                   jax.ShapeDtypeStruct((n_heads, PEER_N_KEYS, n), F32),
                   jax.ShapeDtypeStruct((n_heads, PEER_N_KEYS, n), F32),
                   jax.ShapeDtypeStruct((n_heads, n), F32)],
        scratch_shapes=[pltpu.VMEM((n_sets * PEER_HALF, t), F32),
                        pltpu.VMEM((n_sets, PEER_N_KEYS, t), F32),
                        pltpu.VMEM((n_sets, PEER_TOPK, t), F32)],
        compiler_params=_params(("arbitrary",)),
        name="peer_select",
    )(x, norm, modarr, modarr, wqt, keys)


def _peer_b_kernel(n_heads, n_chunks, ht_ref, u_ref, vt_ref, e1_ref, e0n_ref, tau_ref, x_ref, g_ref, y_ref,
                   acc_sc, *chunk_sc):
    at_sc, w_sc = chunk_sc[:n_chunks], chunk_sc[n_chunks:]
    e = pl.program_id(1)

    @pl.when(e == 0)
    def _():
        acc_sc[...] = jnp.zeros_like(acc_sc)

    ht = ht_ref[...]
    t = ht.shape[1]
    ch = PEER_B_CHUNK
    for c in range(n_chunks):
        at_sc[c][...] = _dot(u_ref[c * ch:(c + 1) * ch, :], ht)
    for c in range(n_chunks):
        for r0 in range(0, ch, 64):
            ii, j0 = divmod(c * ch + r0, PEER_N_KEYS)
            for l0 in range(0, t, LANE):
                ls = slice(l0, l0 + LANE)
                gate = None
                for hh in range(n_heads):
                    p = e1_ref[hh, j0:j0 + 64, ls] * e0n_ref[hh, ii:ii + 1, ls]
                    g = jnp.where(p >= tau_ref[hh:hh + 1, ls], p, 0.0)
                    gate = g if gate is None else gate + g
                a = at_sc[c][r0:r0 + 64, ls]
                th = jnp.tanh(a * (GELU_C0 + GELU_C1 * (a * a)))
                w_sc[c][r0:r0 + 64, ls] = (gate * (a + a * th)).astype(BF16)
        wc = w_sc[c][...]
        for d0 in range(0, acc_sc.shape[0], PEER_OUT_ROWS):
            d1 = min(d0 + PEER_OUT_ROWS, acc_sc.shape[0])
            acc_sc[d0:d1, :] += _dot(vt_ref[d0:d1, c * ch:(c + 1) * ch], wc)

    @pl.when(e == pl.num_programs(1) - 1)
    def _():
        y_ref[...] = x_ref[...] + g_ref[0] * acc_sc[...].T


def _peer_b(ht, u, vt, layer, e1, e0n, tau, x, rows, modarr):
    n, d = x.shape
    t = PEER_B_TILE
    te = PEER_E_TILE
    n_exp = u.shape[1]
    n_heads = e1.shape[0]
    ti = te // PEER_N_KEYS
    n_chunks = te // PEER_B_CHUNK
    return pl.pallas_call(
        functools.partial(_peer_b_kernel, n_heads, n_chunks),
        grid=(n // t, n_exp // te),
        in_specs=[pl.BlockSpec((d, t), lambda i, e: (0, i)),
                  pl.BlockSpec((None, te, d), lambda i, e: (layer, e, 0)),
                  pl.BlockSpec((None, d, te), lambda i, e: (layer, 0, e)),
                  pl.BlockSpec((n_heads, PEER_N_KEYS, t), lambda i, e: (0, 0, i)),
                  pl.BlockSpec((n_heads, ti, t), lambda i, e: (0, e, i)),
                  pl.BlockSpec((n_heads, t), lambda i, e: (0, i)),
                  pl.BlockSpec((t, d), lambda i, e: (i, 0)),
                  pl.BlockSpec((1, 1, d), lambda i, e: (rows.mod_row(i, t) * 6 + 5, 0, 0))],
        out_specs=pl.BlockSpec((t, d), lambda i, e: (i, 0)),
        out_shape=jax.ShapeDtypeStruct((n, d), F32),
        scratch_shapes=([pltpu.VMEM((d, t), F32)] + [pltpu.VMEM((PEER_B_CHUNK, t), F32)] * n_chunks
                        + [pltpu.VMEM((PEER_B_CHUNK, t), BF16)] * n_chunks),
        compiler_params=_params(("arbitrary", "arbitrary")),
        name="peer_experts",
    )(ht, u, vt, e1, e0n, tau, x, modarr)


def _pad_heads(w, n_heads, width):
    k = w.shape[0]
    w = w.reshape(k, n_heads, width)
    return jnp.pad(w, ((0, 0), (0, 0), (0, LANE - width))).reshape(k, n_heads * LANE)


def _pad_lanes(g, lo=0):
    g = g.reshape(1, -1).astype(F32)
    return jnp.pad(g, ((0, 0), (lo, LANE - lo - g.shape[1])))


def kernel(x_prompt, x_sample, cache_da_k, cache_da_v, cache_mla_ckv, cache_mla_kpe, c, c_ctx, norm1, norm2, ada_w, ada_b, da_wq, da_wk, da_wv, da_wo, da_q_norm, da_k_norm, da_lq1, da_lk1, da_lq2, da_lk2, da_subln, mla_wdq, mla_qa_norm, mla_wuq, mla_wdkv, mla_kv_norm, mla_wukv, mla_q_norm, mla_k_norm, mla_wo, peer_wq, peer_keys, peer_u, peer_v):
    bp, sp, d = x_prompt.shape
    bs, ss, _ = x_sample.shape
    depth = norm1.shape[0]
    past = cache_da_k.shape[2]
    n_p = bp * sp
    rows = _Rows(n_p, bs, ss)
    tm = TOK_TILE
    assert sp % tm == 0 and ss % tm == 0 and n_p % ss == 0 and past == sp
    assert n_p % PEER_B_TILE == 0 and ss % PEER_B_TILE == 0

    da_heads = da_wv.shape[2] // DA_V_DIM
    mla_heads = mla_wo.shape[1] // MLA_V_DIM
    kv_lora = mla_kv_norm.shape[1]
    peer_heads = peer_keys.shape[1]

    x = jnp.concatenate([x_prompt.reshape(n_p, d), x_sample.reshape(bs * ss, d)], axis=0)

    n_cond = 1 + bs
    r_pad = -(-n_cond // 8) * 8
    cvecs = jnp.concatenate([c_ctx[None, :], c, jnp.zeros((r_pad - n_cond, d), F32)], axis=0)
    mods = _modulation(cvecs, ada_w, ada_b).reshape(depth, r_pad * 6, 1, d)

    da_cos, da_sin = _rope_tables(ss, tm, 0, DA_HEAD_DIM, DA_HEAD_DIM)
    mla_cos, mla_sin = _rope_tables(ss, tm, MLA_NOPE, MLA_ROPE, LANE)
    ident = ss // tm

    assert tm == sp
    n_da = (depth + 1) // 2
    n_mla = depth // 2
    da_k = da_v = mla_c = mla_p = None
    u_all = peer_u.astype(BF16)
    vt_all = jnp.swapaxes(peer_v, 1, 2).astype(BF16)
    for l in range(depth):
        modarr = mods[l]
        j = l // 2
        if l % 2 == 0:
            lam_init = 0.8 - 0.6 * math.exp(-0.3 * l)
            q, kb, vb, da_k, da_v = _da_qkv(
                x, rows, norm1[l][None, :], modarr,
                da_wq[j].astype(BF16), da_wk[j].astype(BF16), da_wv[j].astype(BF16),
                jnp.tile(da_q_norm[j][None, :], (1, 2)), jnp.tile(da_k_norm[j][None, :], (1, 2)), da_cos, da_sin,
                j, n_da, da_k, da_v, bp, sp)
            lam_vecs = jnp.stack([da_lq1[j], da_lk1[j], da_lq2[j], da_lk2[j]]).astype(F32)
            extra = [lam_vecs, da_subln[j][None, :]]
            extra_specs = [pl.BlockSpec(lam_vecs.shape, lambda b, hp, qi: (0, 0)),
                           pl.BlockSpec((1, DA_V_DIM), lambda b, hp, qi: (0, 0))]
            ck = cache_da_k[:, j].reshape(bs * past, 2 * da_heads * DA_HEAD_DIM)
            cv = cache_da_v[:, j].reshape(bs * past, da_heads * DA_V_DIM)
            o_p = _attention(functools.partial(_da_attn_kernel, lam_init, 1), da_heads, da_heads, LANE, q, sp, 0, bp,
                             [(kb, vb, sp, 0)], extra, extra_specs, "da_attn_prompt")
            o_s = _attention(functools.partial(_da_attn_kernel, lam_init, 2), da_heads, 1, LANE, q, ss, n_p, bs,
                             [(kb, vb, ss, n_p), (ck, cv, past, 0)], extra, extra_specs, "da_attn_latent")
            wo = da_wo[j].astype(BF16)
        else:
            wuq = _pad_heads(mla_wuq[j], mla_heads, MLA_QK_DIM).astype(BF16)
            wdkv = jnp.concatenate([
                mla_wdkv[j][:, :kv_lora],
                jnp.pad(mla_wdkv[j][:, kv_lora:], ((0, 0), (MLA_NOPE, LANE - MLA_QK_DIM)))], axis=1).astype(BF16)
            wukv = mla_wukv[j].reshape(kv_lora, mla_heads, MLA_NOPE + MLA_V_DIM)
            wuk = jnp.pad(wukv[:, :, :MLA_NOPE], ((0, 0), (0, 0), (0, LANE - MLA_NOPE)))
            wuk = wuk.reshape(kv_lora, mla_heads * LANE).astype(BF16)
            wuv = wukv[:, :, MLA_NOPE:].reshape(kv_lora, mla_heads * MLA_V_DIM).astype(BF16)
            q, ckv, kpe, mla_c, mla_p = _mla_q(
                x, rows, norm1[l][None, :], modarr, mla_wdq[j].astype(BF16), mla_qa_norm[j][None, :], wuq,
                _pad_lanes(mla_q_norm[j]), wdkv, mla_kv_norm[j][None, :], mla_cos, mla_sin,
                j, n_mla, mla_c, mla_p, bp, sp)
            kn = _pad_lanes(mla_k_norm[j])
            k, v = _mla_kv(ckv, kpe, wuk, wuv, kn, mla_cos, mla_sin, lambda i: rows.pos_blk(i, tm))
            c_ckv = cache_mla_ckv[:, j].reshape(bs * past, kv_lora)
            c_kpe = jnp.pad(cache_mla_kpe[:, j].reshape(bs * past, MLA_ROPE), ((0, 0), (MLA_NOPE, LANE - MLA_QK_DIM)))
            ck, cv = _mla_kv(c_ckv, c_kpe, wuk, wuv, kn, mla_cos, mla_sin, lambda i: ident)
            o_p = _attention(functools.partial(_mla_attn_kernel, 1), mla_heads // 2, mla_heads // 2, 2 * LANE, q, sp, 0, bp,
                             [(k, v, sp, 0)], [], [], "mla_attn_prompt")
            o_s = _attention(functools.partial(_mla_attn_kernel, 2), mla_heads // 2, 1, 2 * LANE, q, ss, n_p, bs,
                             [(k, v, ss, n_p), (ck, cv, past, 0)], [], [], "mla_attn_latent")
            wo = mla_wo[j].astype(BF16)
        x = _out_proj(o_p, o_s, wo, x, rows, modarr, 2)

        wqt = peer_wq[l].T.astype(BF16)
        keys = peer_keys[l].reshape(2 * peer_heads, PEER_N_KEYS, PEER_HALF).astype(BF16)
        ht, e1, e0n, tau = _peer_a(x, rows, norm2[l][None, :], modarr, wqt, keys)
        x = _peer_b(ht, u_all, vt_all, l, e1, e0n, tau, x, rows, modarr)

    return (x[:n_p].reshape(bp, sp, d), x[n_p:].reshape(bs, ss, d),
            da_k.reshape(bp, n_da, sp, 2 * da_heads, DA_HEAD_DIM), da_v.reshape(bp, n_da, sp, da_heads, DA_V_DIM),
            mla_c, mla_p)
```

```python
import functools
import math

import jax
import jax.numpy as jnp
from jax import lax
from jax.experimental import pallas as pl
from jax.experimental.pallas import tpu as pltpu

GRID_W = 64
DA_HEAD_DIM = 64
DA_V_DIM = 2 * DA_HEAD_DIM
MLA_NOPE = 64
MLA_ROPE = 32
MLA_QK_DIM = MLA_NOPE + MLA_ROPE
MLA_V_DIM = 64
PEER_N_KEYS = 128
PEER_HALF = 128
PEER_TOPK = 16
ROPE_BASE = 10000.0
EPS = 1e-6

LANE = 128
VMEM_LIMIT = 56 * 1024 * 1024
TOK_TILE = 256
Q_TILE = 1024
KEY_CHUNK = 1024
LOG2E = math.log2(math.e)
PEER_A_TILE = 256
PEER_B_TILE = 512
PEER_E_TILE = 2048
PEER_B_CHUNK = 256
PEER_OUT_ROWS = 512
GELU_C0 = math.sqrt(2.0 / math.pi)
GELU_C1 = GELU_C0 * 0.044715

BF16 = jnp.bfloat16
F32 = jnp.float32


def _params(sem, flags=None):
    return pltpu.CompilerParams(dimension_semantics=sem, vmem_limit_bytes=VMEM_LIMIT, flags=flags)


def _dot(a, b):
    return jnp.dot(a, b, preferred_element_type=F32)


def _dot_nt(a, b):
    return lax.dot_general(a, b, (((1,), (1,)), ((), ())), preferred_element_type=F32)


def _rms(x):
    return x * lax.rsqrt(jnp.mean(x * x, axis=-1, keepdims=True) + EPS)


def _modulate(x, g, shift, scale):
    return _rms(x) * g * (1.0 + scale) + shift


def _rope(x, cos, sin, half):
    lane = lax.broadcasted_iota(jnp.int32, x.shape, 1)
    first = (lane & (2 * half - 1)) < half
    partner = jnp.where(first, pltpu.roll(x, LANE - half, 1), pltpu.roll(x, half, 1))
    return x * cos + partner * sin


def _segments(seg):
    lane = jnp.arange(LANE) // seg
    return (lane[:, None] == lane[None, :]).astype(BF16)


def _head_norm128(t, g, width, seg):
    ms = _dot((t * t).astype(BF16), seg) * (1.0 / width)
    return t * lax.rsqrt(ms + EPS) * g


class _Rows:
    def __init__(self, n_p, n_b, s_lat):
        self.n_p, self.n_b, self.s_lat = n_p, n_b, s_lat
        self.n = n_p + n_b * s_lat

    def mod_row(self, i, tm):
        np_t = self.n_p // tm
        return jnp.where(i < np_t, 0, 1 + (i - np_t) // (self.s_lat // tm))

    def pos_blk(self, i, tm):
        np_t = self.n_p // tm
        return jnp.where(i < np_t, self.s_lat // tm, (i - np_t) % (self.s_lat // tm))


def _mod_spec(rows, tm, which, d):
    return pl.BlockSpec((1, 1, d), lambda i: (rows.mod_row(i, tm) * 6 + which, 0, 0))


def _rope_tables(s_lat, tm, rot_lo, rot_w, period):
    hd = rot_w // 2
    half = hd // 2
    lane = jnp.arange(LANE)
    d = (lane % period) - rot_lo
    rot = (d >= 0) & (d < rot_w)
    d = jnp.clip(d, 0, rot_w - 1)
    use_row = d < hd
    dd = d % hd
    i = dd % half
    first = dd < half
    inv = 1.0 / (ROPE_BASE ** ((2 * i).astype(F32) / hd))
    t = jnp.arange(s_lat)
    row = (t // GRID_W).astype(F32)
    col = (t % GRID_W).astype(F32)
    pos = jnp.where(use_row[None, :], row[:, None], col[:, None])
    ang = pos * inv[None, :]
    cos = jnp.where(rot[None, :], jnp.cos(ang), 1.0)
    sin = jnp.where(rot[None, :], jnp.where(first[None, :], -jnp.sin(ang), jnp.sin(ang)), 0.0)
    cos = jnp.concatenate([cos, jnp.ones((tm, LANE), F32)], axis=0)
    sin = jnp.concatenate([sin, jnp.zeros((tm, LANE), F32)], axis=0)
    return cos.astype(F32), sin.astype(F32)


def _cache_out(prev, shape, blk_tail, rows, tm, layer):
    np_t = rows.n_p // tm
    spec = pl.BlockSpec((None, None) + blk_tail, lambda i: (jnp.minimum(i, np_t - 1), layer, 0, 0))
    sds = jax.ShapeDtypeStruct(shape, F32)
    if prev is None:
        prev = jnp.zeros(shape, F32)
    return spec, sds, [prev], [pl.BlockSpec(memory_space=pl.ANY)]


def _mod_kernel(c_ref, w_ref, b_ref, o_ref):
    c = c_ref[...]
    s = c * jax.nn.sigmoid(c)
    o_ref[...] = _dot(s.astype(BF16), w_ref[...].astype(BF16)) + b_ref[...]


def _modulation(cvecs, ada_w, ada_b):
    depth, d, d6 = ada_w.shape
    r = cvecs.shape[0]
    tn = d
    return pl.pallas_call(
        _mod_kernel,
        grid=(depth, d6 // tn),
        in_specs=[
            pl.BlockSpec((r, d), lambda l, j: (0, 0)),
            pl.BlockSpec((None, d, tn), lambda l, j: (l, 0, j)),
            pl.BlockSpec((None, 1, tn), lambda l, j: (l, 0, j)),
        ],
        out_specs=pl.BlockSpec((None, r, tn), lambda l, j: (l, 0, j)),
        out_shape=jax.ShapeDtypeStruct((depth, r, d6), F32),
        compiler_params=_params(("arbitrary", "arbitrary")),
        name="adaln_modulation",
    )(cvecs, ada_w, ada_b.reshape(depth, 1, d6))


def _da_qkv_kernel(n_prompt_tiles, n_prev, x_ref, n_ref, sh_ref, sc_ref, wq_ref, wk_ref, wv_ref, qn_ref, kn_ref,
                   seg_ref, cos_ref, sin_ref, *refs):
    q_ref, kb_ref, vb_ref, kf_ref, vf_ref = refs[n_prev:]
    h = _modulate(x_ref[...], n_ref[...], sh_ref[0], sc_ref[0]).astype(BF16)
    q = _dot(h, wq_ref[...])
    k = _dot(h, wk_ref[...])
    v = _dot(h, wv_ref[...])
    vb_ref[...] = v.astype(BF16)
    cos = cos_ref[...]
    sin = sin_ref[...]
    seg = seg_ref[...]

    def head_norm(t, g):
        return _head_norm128(t, g, DA_HEAD_DIM, seg)

    scale = DA_HEAD_DIM ** -0.5 * LOG2E
    ks = []
    for j in range(q.shape[1] // LANE):
        sl = slice(j * LANE, (j + 1) * LANE)
        qj = _rope(head_norm(q[:, sl], qn_ref[...]), cos, sin, DA_HEAD_DIM // 4)
        kj = _rope(head_norm(k[:, sl], kn_ref[...]), cos, sin, DA_HEAD_DIM // 4)
        q_ref[:, sl] = (qj * scale).astype(BF16)
        kb_ref[:, sl] = kj.astype(BF16)
        ks.append(kj)

    @pl.when(pl.program_id(0) < n_prompt_tiles)
    def _():
        vf_ref[...] = v
        for j, kj in enumerate(ks):
            kf_ref[:, j * LANE:(j + 1) * LANE] = kj


def _da_qkv(x, rows, norm, modarr, wq, wk, wv, qn, kn, cos, sin, layer, n_layers, prev_k, prev_v, bp, sp):
    n, d = x.shape
    tm = TOK_TILE
    w = wq.shape[1]
    full = lambda shape: pl.BlockSpec(shape, lambda i: (0,) * len(shape))
    tok = lambda width: pl.BlockSpec((tm, width), lambda i: (i, 0))
    tab = pl.BlockSpec((tm, LANE), lambda i: (rows.pos_blk(i, tm), 0))
    k_spec, k_sds, k_in, k_in_specs = _cache_out(prev_k, (bp, n_layers, sp, w), (sp, w), rows, tm, layer)
    v_spec, v_sds, v_in, v_in_specs = _cache_out(prev_v, (bp, n_layers, sp, w), (sp, w), rows, tm, layer)
    n_prev = len(k_in) + len(v_in)
    n_in = 12
    return pl.pallas_call(
        functools.partial(_da_qkv_kernel, rows.n_p // tm, n_prev),
        grid=(n // tm,),
        in_specs=[tok(d), full((1, d)), _mod_spec(rows, tm, 0, d), _mod_spec(rows, tm, 1, d),
                  full((d, w)), full((d, w)), full((d, w)), full((1, LANE)), full((1, LANE)), full((LANE, LANE)),
                  tab, tab] + k_in_specs + v_in_specs,
        out_specs=[tok(w)] * 3 + [k_spec, v_spec],
        out_shape=[jax.ShapeDtypeStruct((n, w), BF16)] * 3 + [k_sds, v_sds],
        input_output_aliases={n_in + a: 3 + a for a in range(n_prev)},
        compiler_params=_params(("arbitrary",)),
        name="da_qkv",
    )(x, norm, modarr, modarr, wq, wk, wv, qn, kn, _segments(DA_HEAD_DIM), cos, sin, *k_in, *v_in)


def _attend(qs, ksls, vsl, kv_refs):
    n = len(qs)
    m, acc = [None] * n, [None] * n
    for k_ref, v_ref in kv_refs:
        sk = k_ref.shape[0]
        step = min(KEY_CHUNK, sk)
        for c0 in range(0, sk, step):
            v = v_ref[c0:c0 + step, vsl].astype(BF16)
            v1 = jnp.concatenate([v, jnp.ones_like(v)], axis=1)
            for a in range(n):
                s = _dot_nt(qs[a], k_ref[c0:c0 + step, ksls[a]].astype(BF16))
                mx = s.max(axis=-1, keepdims=True)
                if m[a] is None:
                    m[a] = mx
                    acc[a] = _dot(jnp.exp2((s - mx).astype(BF16)), v1)
                else:
                    m_new = jnp.maximum(m[a], mx)
                    alpha = jnp.exp2(m[a] - m_new)
                    acc[a] = alpha * acc[a] + _dot(jnp.exp2((s - m_new).astype(BF16)), v1)
                    m[a] = m_new
    return [acc[a][:, :LANE] / acc[a][:, LANE:] for a in range(n)]


def _da_attn_kernel(lam_init, n_chunks, group, lam_ref, subln_ref, q_ref, *refs):
    kv = [(refs[2 * c], refs[2 * c + 1]) for c in range(n_chunks)]
    o_ref = refs[2 * n_chunks]
    lv = lam_ref[...]
    lam = (jnp.exp(jnp.sum(lv[0:1] * lv[1:2], axis=-1, keepdims=True))
           - jnp.exp(jnp.sum(lv[2:3] * lv[3:4], axis=-1, keepdims=True)) + lam_init)
    for g in range(group):
        sl = slice(g * LANE, (g + 1) * LANE)
        q = q_ref[:, sl]
        lane = lax.broadcasted_iota(jnp.int32, q.shape, 1)
        zero = jnp.zeros_like(q)
        qs = [jnp.where(lane < DA_HEAD_DIM, q, zero), jnp.where(lane < DA_HEAD_DIM, zero, q)]
        o1, o2 = _attend(qs, [sl] * 2, sl, kv)
        o = o1 - lam * o2
        o = _rms(o) * subln_ref[...] * (1.0 - lam_init)
        o_ref[:, sl] = o.astype(o_ref.dtype)


def _mla_attn_kernel(n_chunks, group, q_ref, *refs):
    kv = [(refs[2 * c], refs[2 * c + 1]) for c in range(n_chunks)]
    o_ref = refs[2 * n_chunks]
    for g in range(group):
        sls = [slice((2 * g + half) * LANE, (2 * g + half + 1) * LANE) for half in range(2)]
        vsl = slice(g * LANE, (g + 1) * LANE)
        outs = _attend([q_ref[:, sl] for sl in sls], sls, vsl, kv)
        lane = lax.broadcasted_iota(jnp.int32, outs[0].shape, 1)
        o_ref[:, vsl] = jnp.where(lane < MLA_V_DIM, outs[0], outs[1]).astype(o_ref.dtype)


def _attention(kernel, n_pairs, group, qw, q, seq, row_off, n_b, chunks, extra_in, extra_specs, name):
    tq = min(Q_TILE, seq)
    nq = seq // tq
    q_spec = pl.BlockSpec((tq, qw * group), lambda b, hp, qi: (row_off // tq + b * nq + qi, hp))
    in_specs = list(extra_specs) + [q_spec]
    args = list(extra_in) + [q]
    for k, v, sk, off in chunks:
        kv_map = functools.partial(lambda b, hp, qi, o, s: (o // s + b, hp), o=off, s=sk)
        in_specs.append(pl.BlockSpec((sk, qw * group), kv_map))
        in_specs.append(pl.BlockSpec((sk, LANE * group), kv_map))
        args += [k, v]
    return pl.pallas_call(
        functools.partial(kernel, group),
        grid=(n_b, n_pairs // group, nq),
        in_specs=in_specs,
        out_specs=pl.BlockSpec((tq, LANE * group), lambda b, hp, qi: (b * nq + qi, hp)),
        out_shape=jax.ShapeDtypeStruct((n_b * seq, n_pairs * LANE), BF16),
        compiler_params=_params(("arbitrary", "arbitrary", "arbitrary")),
        name=name,
    )(*args)


def _out_proj_kernel(n_prompt_tiles, op_ref, os_ref, w_ref, x_ref, g_ref, y_ref):
    o = jnp.where(pl.program_id(0) < n_prompt_tiles, op_ref[...], os_ref[...])
    y_ref[...] = x_ref[...] + g_ref[0] * _dot(o, w_ref[...])


def _out_proj(o_p, o_s, w, x, rows, modarr, which):
    n, d = x.shape
    tm = TOK_TILE
    k = o_p.shape[1]
    np_t = rows.n_p // tm
    return pl.pallas_call(
        functools.partial(_out_proj_kernel, np_t),
        grid=(n // tm,),
        in_specs=[pl.BlockSpec((tm, k), lambda i: (jnp.minimum(i, np_t - 1), 0)),
                  pl.BlockSpec((tm, k), lambda i: (jnp.maximum(i - np_t, 0), 0)),
                  pl.BlockSpec((k, d), lambda i: (0, 0)),
                  pl.BlockSpec((tm, d), lambda i: (i, 0)), _mod_spec(rows, tm, which, d)],
        out_specs=pl.BlockSpec((tm, d), lambda i: (i, 0)),
        out_shape=jax.ShapeDtypeStruct((n, d), F32),
        compiler_params=_params(("arbitrary",)),
        name="out_proj",
    )(o_p, o_s, w, x, modarr)


def _mla_q_kernel(kv_lora, n_prompt_tiles, n_prev, x_ref, n_ref, sh_ref, sc_ref, wdq_ref, qan_ref, wuq_ref, qn_ref,
                  wdkv_ref, kvn_ref, seg_ref, cos_ref, sin_ref, *refs):
    q_ref, ckv_ref, kpe_ref, ckvp_ref, kpep_ref = refs[n_prev:]
    h = _modulate(x_ref[...], n_ref[...], sh_ref[0], sc_ref[0]).astype(BF16)
    cq = (_rms(_dot(h, wdq_ref[...])) * qan_ref[...]).astype(BF16)
    q = _dot(cq, wuq_ref[...])
    cos = cos_ref[...]
    sin = sin_ref[...]
    scale = MLA_QK_DIM ** -0.5 * LOG2E
    for j in range(q.shape[1] // LANE):
        sl = slice(j * LANE, (j + 1) * LANE)
        qj = _rope(_head_norm128(q[:, sl], qn_ref[...], MLA_QK_DIM, seg_ref[...]), cos, sin, MLA_ROPE // 4)
        q_ref[:, sl] = (qj * scale).astype(BF16)
    ckvf = _dot(h, wdkv_ref[...])
    ckv = _rms(ckvf[:, :kv_lora]) * kvn_ref[...]
    ckv_ref[...] = ckv
    kpe_ref[...] = ckvf[:, kv_lora:]

    @pl.when(pl.program_id(0) < n_prompt_tiles)
    def _():
        ckvp_ref[...] = ckv
        kpep_ref[...] = ckvf[:, kv_lora + MLA_NOPE:kv_lora + MLA_QK_DIM]


def _mla_q(x, rows, norm, modarr, wdq, qan, wuq, qn, wdkv, kvn, cos, sin, layer, n_layers, prev_c, prev_p, bp, sp):
    n, d = x.shape
    tm = TOK_TILE
    ql = wdq.shape[1]
    qw = wuq.shape[1]
    kv_lora = kvn.shape[1]
    full = lambda shape: pl.BlockSpec(shape, lambda i: (0,) * len(shape))
    tok = lambda width: pl.BlockSpec((tm, width), lambda i: (i, 0))
    tab = pl.BlockSpec((tm, LANE), lambda i: (rows.pos_blk(i, tm), 0))
    c_spec, c_sds, c_in, c_in_specs = _cache_out(prev_c, (bp, n_layers, sp, kv_lora), (sp, kv_lora), rows, tm, layer)
    p_spec, p_sds, p_in, p_in_specs = _cache_out(prev_p, (bp, n_layers, sp, MLA_ROPE), (sp, MLA_ROPE), rows, tm, layer)
    n_prev = len(c_in) + len(p_in)
    n_in = 13
    return pl.pallas_call(
        functools.partial(_mla_q_kernel, kv_lora, rows.n_p // tm, n_prev),
        grid=(n // tm,),
        in_specs=[tok(d), full((1, d)), _mod_spec(rows, tm, 0, d), _mod_spec(rows, tm, 1, d),
                  full((d, ql)), full((1, ql)), full((ql, qw)), full((1, LANE)),
                  full((d, kv_lora + LANE)), full((1, kv_lora)), full((LANE, LANE)), tab, tab]
        + c_in_specs + p_in_specs,
        out_specs=[tok(qw), tok(kv_lora), tok(LANE), c_spec, p_spec],
        out_shape=[jax.ShapeDtypeStruct((n, qw), BF16), jax.ShapeDtypeStruct((n, kv_lora), F32),
                   jax.ShapeDtypeStruct((n, LANE), F32), c_sds, p_sds],
        input_output_aliases={n_in + a: 3 + a for a in range(n_prev)},
        compiler_params=_params(("arbitrary",)),
        name="mla_q_ckv",
    )(x, norm, modarr, modarr, wdq, qan, wuq, qn, wdkv, kvn, _segments(LANE), cos, sin, *c_in, *p_in)


def _mla_kv_kernel(ckv_ref, kpe_ref, wuk_ref, wuv_ref, kn_ref, seg_ref, cos_ref, sin_ref, k_ref, v_ref):
    c = ckv_ref[...].astype(BF16)
    kn = _dot(c, wuk_ref[...])
    v_ref[...] = _dot(c, wuv_ref[...]).astype(BF16)
    kpe = kpe_ref[...]
    cos = cos_ref[...]
    sin = sin_ref[...]
    for j in range(kn.shape[1] // LANE):
        sl = slice(j * LANE, (j + 1) * LANE)
        kj = _rope(_head_norm128(kn[:, sl] + kpe, kn_ref[...], MLA_QK_DIM, seg_ref[...]), cos, sin, MLA_ROPE // 4)
        k_ref[:, sl] = kj.astype(BF16)


def _mla_kv(ckv, kpe, wuk, wuv, kn, cos, sin, pos_blk):
    n, kv_lora = ckv.shape
    tm = TOK_TILE
    kw = wuk.shape[1]
    vw = wuv.shape[1]
    full = lambda shape: pl.BlockSpec(shape, lambda i: (0,) * len(shape))
    tok = lambda width: pl.BlockSpec((tm, width), lambda i: (i, 0))
    tab = pl.BlockSpec((tm, LANE), lambda i: (pos_blk(i), 0))
    return pl.pallas_call(
        _mla_kv_kernel,
        grid=(n // tm,),
        in_specs=[tok(kv_lora), tok(LANE), full((kv_lora, kw)), full((kv_lora, vw)), full((1, LANE)),
                  full((LANE, LANE)), tab, tab],
        out_specs=[tok(kw), tok(vw)],
        out_shape=[jax.ShapeDtypeStruct((n, kw), BF16), jax.ShapeDtypeStruct((n, vw), BF16)],
        compiler_params=_params(("arbitrary",)),
        name="mla_kv_expand",
    )(ckv, kpe, wuk, wuv, kn, _segments(LANE), cos, sin)


def _sort_pairs(n):
    pairs = []
    p = 1
    while p < n:
        k = p
        while k >= 1:
            for j in range(k % p, n - k, 2 * k):
                for i in range(min(k, n - j - k)):
                    if (i + j) // (2 * p) == (i + j + k) // (2 * p):
                        pairs.append((i + j, i + j + k))
            k //= 2
        p *= 2
    return pairs


def _cmpx(x, i, j):
    a, b = x[i], x[j]
    if b is None:
        return
    if a is None:
        x[i], x[j] = b, None
        return
    x[i], x[j] = jnp.maximum(a, b), jnp.minimum(a, b)


def _top_sorted(x):
    k = PEER_TOPK
    x = list(x)
    for i, j in _sort_pairs(k):
        _cmpx(x, i, j)
    shift = 4
    while shift >= 1:
        y = []
        for i in range(k):
            other = x[k - 1 - i]
            other = None if other is None else pltpu.roll(other, 8 - shift, 0)
            if x[i] is None:
                y.append(other)
            elif other is None:
                y.append(x[i])
            else:
                y.append(jnp.maximum(x[i], other))
        d = k // 2
        while d >= 1:
            for i in range(k):
                if (i & d) == 0:
                    _cmpx(y, i, i + d)
            d //= 2
        x = y
        shift //= 2
    return x


def _pair_candidates(a_ref, b_ref, scale):
    k = PEER_TOPK
    b_lo = b_ref[0:8, :]
    sub = lax.broadcasted_iota(jnp.int32, b_lo.shape, 0)
    row = lambda r: a_ref[r:r + 1, :] * scale
    out = [row(0) * b_lo, row(0) * b_ref[8:16, :]]
    for r in range(1, 8):
        n_r = k // (r + 1)
        c = row(r) * b_lo
        out.append(c if n_r >= 8 else jnp.where(sub < n_r, c, -1.0))
    out.append((a_ref[8:16, :] * scale) * b_ref[0:1, :])
    return out + [None] * (k - len(out))


def _peer_a_kernel(n_heads, x_ref, n_ref, sh_ref, sc_ref, wqt_ref, keys_ref,
                   ht_ref, e1_ref, e0n_ref, tau_ref, qt_sc, e_sc, top_sc):
    k = PEER_TOPK
    h = _modulate(x_ref[...], n_ref[...], sh_ref[0], sc_ref[0])
    ht = h.T.astype(BF16)
    ht_ref[...] = ht
    qt_sc[...] = _dot(wqt_ref[...], ht)

    for hc in range(2 * n_heads):
        qb = qt_sc[hc * PEER_HALF:(hc + 1) * PEER_HALF, :]
        s = _dot(keys_ref[hc], qb.astype(BF16))
        e_sc[hc] = jnp.exp(s - jnp.max(s, axis=0, keepdims=True))

    def per_set(hc, carry):
        e = e_sc[hc]
        top = _top_sorted([e[g * 8:(g + 1) * 8, :] for g in range(PEER_N_KEYS // 8)])
        for r in range(k):
            top_sc[hc, r:r + 1, :] = top[r][0:1, :]
        return carry

    lax.fori_loop(0, 2 * n_heads, per_set, 0)

    def per_head(hh, carry):
        a_ref = top_sc.at[2 * hh]
        b_ref = top_sc.at[2 * hh + 1]
        cands = _pair_candidates(a_ref, b_ref, 1.0)
        top = _top_sorted(cands)
        z = top[0][0:1, :]
        for r in range(1, k):
            z = z + top[r][0:1, :]
        rz = 0.5 / z
        e0n_ref[hh] = e_sc[2 * hh] * rz
        e1_ref[hh] = e_sc[2 * hh + 1]
        kth = top[k - 1][0:1, :]
        taun = None
        for cu, cn in zip(cands, _pair_candidates(a_ref, b_ref, rz)):
            if cu is not None:
                v = jnp.where(cu == kth, cn, jnp.inf)
                taun = v if taun is None else jnp.minimum(taun, v)
        tau_ref[pl.ds(hh, 1), :] = jnp.min(taun, axis=0, keepdims=True)
        return carry

    lax.fori_loop(0, n_heads, per_head, 0)


def _peer_a(x, rows, norm, modarr, wqt, keys):
    n, d = x.shape
    t = PEER_A_TILE
    n_sets = keys.shape[0]
    n_heads = n_sets // 2
    full = lambda shape: pl.BlockSpec(shape, lambda i: (0,) * len(shape))
    return pl.pallas_call(
        functools.partial(_peer_a_kernel, n_heads),
        grid=(n // t,),
        in_specs=[pl.BlockSpec((t, d), lambda i: (i, 0)), full((1, d)),
                  _mod_spec(rows, t, 3, d), _mod_spec(rows, t, 4, d),
                  full(wqt.shape), full(keys.shape)],
        out_specs=[pl.BlockSpec((d, t), lambda i: (0, i)),
                   pl.BlockSpec((n_heads, PEER_N_KEYS, t), lambda i: (0, 0, i)),
                   pl.BlockSpec((n_heads, PEER_N_KEYS, t), lambda i: (0, 0, i)),
                   pl.BlockSpec((n_heads, t), lambda i: (0, i))],
        out_shape=[jax.ShapeDtypeStruct((d, n), BF16),
                   jax.ShapeDtypeStruct((n_heads, PEER_N_KEYS, n), F32),
                   jax.ShapeDtypeStruct((n_heads, PEER_N_KEYS, n), F32),
                   jax.ShapeDtypeStruct((n_heads, n), F32)],
        scratch_shapes=[pltpu.VMEM((n_sets * PEER_HALF, t), F32),
                        pltpu.VMEM((n_sets, PEER_N_KEYS, t), F32),
                        pltpu.VMEM((n_sets, PEER_TOPK, t), F32)],
        compiler_params=_params(("arbitrary",)),
        name="peer_select",
    )(x, norm, modarr, modarr, wqt, keys)


def _peer_b_kernel(n_heads, n_chunks, ht_ref, u_ref, vt_ref, e1_ref, e0n_ref, tau_ref, x_ref, g_ref, y_ref,
                   acc_sc, *chunk_sc):
    at_sc, w_sc = chunk_sc[:n_chunks], chunk_sc[n_chunks:]
    e = pl.program_id(1)

    @pl.when(e == 0)
    def _():
        acc_sc[...] = jnp.zeros_like(acc_sc)

    ht = ht_ref[...]
    t = ht.shape[1]
    ch = PEER_B_CHUNK
    for c in range(n_chunks):
        at_sc[c][...] = _dot(u_ref[c * ch:(c + 1) * ch, :], ht)
    for c in range(n_chunks):
        for r0 in range(0, ch, 64):
            ii, j0 = divmod(c * ch + r0, PEER_N_KEYS)
            for l0 in range(0, t, LANE):
                ls = slice(l0, l0 + LANE)
                gate = None
                for hh in range(n_heads):
                    p = e1_ref[hh, j0:j0 + 64, ls] * e0n_ref[hh, ii:ii + 1, ls]
                    g = jnp.where(p >= tau_ref[hh:hh + 1, ls], p, 0.0)
                    gate = g if gate is None else gate + g
                a = at_sc[c][r0:r0 + 64, ls]
                th = jnp.tanh(a * (GELU_C0 + GELU_C1 * (a * a)))
                w_sc[c][r0:r0 + 64, ls] = (gate * (a + a * th)).astype(BF16)
        wc = w_sc[c][...]
        for d0 in range(0, acc_sc.shape[0], PEER_OUT_ROWS):
            d1 = min(d0 + PEER_OUT_ROWS, acc_sc.shape[0])
            acc_sc[d0:d1, :] += _dot(vt_ref[d0:d1, c * ch:(c + 1) * ch], wc)

    @pl.when(e == pl.num_programs(1) - 1)
    def _():
        y_ref[...] = x_ref[...] + g_ref[0] * acc_sc[...].T


def _peer_b(ht, u, vt, layer, e1, e0n, tau, x, rows, modarr):
    n, d = x.shape
    t = PEER_B_TILE
    te = PEER_E_TILE
    n_exp = u.shape[1]
    n_heads = e1.shape[0]
    ti = te // PEER_N_KEYS
    n_chunks = te // PEER_B_CHUNK
    return pl.pallas_call(
        functools.partial(_peer_b_kernel, n_heads, n_chunks),
        grid=(n // t, n_exp // te),
        in_specs=[pl.BlockSpec((d, t), lambda i, e: (0, i)),
                  pl.BlockSpec((None, te, d), lambda i, e: (layer, e, 0)),
                  pl.BlockSpec((None, d, te), lambda i, e: (layer, 0, e)),
                  pl.BlockSpec((n_heads, PEER_N_KEYS, t), lambda i, e: (0, 0, i)),
                  pl.BlockSpec((n_heads, ti, t), lambda i, e: (0, e, i)),
                  pl.BlockSpec((n_heads, t), lambda i, e: (0, i)),
                  pl.BlockSpec((t, d), lambda i, e: (i, 0)),
                  pl.BlockSpec((1, 1, d), lambda i, e: (rows.mod_row(i, t) * 6 + 5, 0, 0))],
        out_specs=pl.BlockSpec((t, d), lambda i, e: (i, 0)),
        out_shape=jax.ShapeDtypeStruct((n, d), F32),
        scratch_shapes=([pltpu.VMEM((d, t), F32)] + [pltpu.VMEM((PEER_B_CHUNK, t), F32)] * n_chunks
                        + [pltpu.VMEM((PEER_B_CHUNK, t), BF16)] * n_chunks),
        compiler_params=_params(("arbitrary", "arbitrary")),
        name="peer_experts",
    )(ht, u, vt, e1, e0n, tau, x, modarr)


def _pad_heads(w, n_heads, width):
    k = w.shape[0]
    w = w.reshape(k, n_heads, width)
    return jnp.pad(w, ((0, 0), (0, 0), (0, LANE - width))).reshape(k, n_heads * LANE)


def _pad_lanes(g, lo=0):
    g = g.reshape(1, -1).astype(F32)
    return jnp.pad(g, ((0, 0), (lo, LANE - lo - g.shape[1])))


def kernel(x_prompt, x_sample, cache_da_k, cache_da_v, cache_mla_ckv, cache_mla_kpe, c, c_ctx, norm1, norm2, ada_w, ada_b, da_wq, da_wk, da_wv, da_wo, da_q_norm, da_k_norm, da_lq1, da_lk1, da_lq2, da_lk2, da_subln, mla_wdq, mla_qa_norm, mla_wuq, mla_wdkv, mla_kv_norm, mla_wukv, mla_q_norm, mla_k_norm, mla_wo, peer_wq, peer_keys, peer_u, peer_v):
    bp, sp, d = x_prompt.shape
    bs, ss, _ = x_sample.shape
    depth = norm1.shape[0]
    past = cache_da_k.shape[2]
    n_p = bp * sp
    rows = _Rows(n_p, bs, ss)
    tm = TOK_TILE
    assert sp % tm == 0 and ss % tm == 0 and n_p % ss == 0 and past == sp
    assert n_p % PEER_B_TILE == 0 and ss % PEER_B_TILE == 0

    da_heads = da_wv.shape[2] // DA_V_DIM
    mla_heads = mla_wo.shape[1] // MLA_V_DIM
    kv_lora = mla_kv_norm.shape[1]
    peer_heads = peer_keys.shape[1]

    x = jnp.concatenate([x_prompt.reshape(n_p, d), x_sample.reshape(bs * ss, d)], axis=0)

    n_cond = 1 + bs
    r_pad = -(-n_cond // 8) * 8
    cvecs = jnp.concatenate([c_ctx[None, :], c, jnp.zeros((r_pad - n_cond, d), F32)], axis=0)
    mods = _modulation(cvecs, ada_w, ada_b).reshape(depth, r_pad * 6, 1, d)

    da_cos, da_sin = _rope_tables(ss, tm, 0, DA_HEAD_DIM, DA_HEAD_DIM)
    mla_cos, mla_sin = _rope_tables(ss, tm, MLA_NOPE, MLA_ROPE, LANE)
    ident = ss // tm

    assert tm == sp
    n_da = (depth + 1) // 2
    n_mla = depth // 2
    da_k = da_v = mla_c = mla_p = None
    u_all = peer_u.astype(BF16)
    vt_all = jnp.swapaxes(peer_v, 1, 2).astype(BF16)
    for l in range(depth):
        modarr = mods[l]
        j = l // 2
        if l % 2 == 0:
            lam_init = 0.8 - 0.6 * math.exp(-0.3 * l)
            q, kb, vb, da_k, da_v = _da_qkv(
                x, rows, norm1[l][None, :], modarr,
                da_wq[j].astype(BF16), da_wk[j].astype(BF16), da_wv[j].astype(BF16),
                jnp.tile(da_q_norm[j][None, :], (1, 2)), jnp.tile(da_k_norm[j][None, :], (1, 2)), da_cos, da_sin,
                j, n_da, da_k, da_v, bp, sp)
            lam_vecs = jnp.stack([da_lq1[j], da_lk1[j], da_lq2[j], da_lk2[j]]).astype(F32)
            extra = [lam_vecs, da_subln[j][None, :]]
            extra_specs = [pl.BlockSpec(lam_vecs.shape, lambda b, hp, qi: (0, 0)),
                           pl.BlockSpec((1, DA_V_DIM), lambda b, hp, qi: (0, 0))]
            ck = cache_da_k[:, j].reshape(bs * past, 2 * da_heads * DA_HEAD_DIM)
            cv = cache_da_v[:, j].reshape(bs * past, da_heads * DA_V_DIM)
            o_p = _attention(functools.partial(_da_attn_kernel, lam_init, 1), da_heads, da_heads, LANE, q, sp, 0, bp,
                             [(kb, vb, sp, 0)], extra, extra_specs, "da_attn_prompt")
            o_s = _attention(functools.partial(_da_attn_kernel, lam_init, 2), da_heads, 1, LANE, q, ss, n_p, bs,
                             [(kb, vb, ss, n_p), (ck, cv, past, 0)], extra, extra_specs, "da_attn_latent")
            wo = da_wo[j].astype(BF16)
        else:
            wuq = _pad_heads(mla_wuq[j], mla_heads, MLA_QK_DIM).astype(BF16)
            wdkv = jnp.concatenate([
                mla_wdkv[j][:, :kv_lora],
                jnp.pad(mla_wdkv[j][:, kv_lora:], ((0, 0), (MLA_NOPE, LANE - MLA_QK_DIM)))], axis=1).astype(BF16)
            wukv = mla_wukv[j].reshape(kv_lora, mla_heads, MLA_NOPE + MLA_V_DIM)
            wuk = jnp.pad(wukv[:, :, :MLA_NOPE], ((0, 0), (0, 0), (0, LANE - MLA_NOPE)))
            wuk = wuk.reshape(kv_lora, mla_heads * LANE).astype(BF16)
            wuv = wukv[:, :, MLA_NOPE:].reshape(kv_lora, mla_heads * MLA_V_DIM).astype(BF16)
            q, ckv, kpe, mla_c, mla_p = _mla_q(
                x, rows, norm1[l][None, :], modarr, mla_wdq[j].astype(BF16), mla_qa_norm[j][None, :], wuq,
                _pad_lanes(mla_q_norm[j]), wdkv, mla_kv_norm[j][None, :], mla_cos, mla_sin,
                j, n_mla, mla_c, mla_p, bp, sp)
            kn = _pad_lanes(mla_k_norm[j])
            k, v = _mla_kv(ckv, kpe, wuk, wuv, kn, mla_cos, mla_sin, lambda i: rows.pos_blk(i, tm))
            c_ckv = cache_mla_ckv[:, j].reshape(bs * past, kv_lora)
            c_kpe = jnp.pad(cache_mla_kpe[:, j].reshape(bs * past, MLA_ROPE), ((0, 0), (MLA_NOPE, LANE - MLA_QK_DIM)))
            ck, cv = _mla_kv(c_ckv, c_kpe, wuk, wuv, kn, mla_cos, mla_sin, lambda i: ident)
            o_p = _attention(functools.partial(_mla_attn_kernel, 1), mla_heads // 2, mla_heads // 2, 2 * LANE, q, sp, 0, bp,
                             [(k, v, sp, 0)], [], [], "mla_attn_prompt")
            o_s = _attention(functools.partial(_mla_attn_kernel, 2), mla_heads // 2, 1, 2 * LANE, q, ss, n_p, bs,
                             [(k, v, ss, n_p), (ck, cv, past, 0)], [], [], "mla_attn_latent")
            wo = mla_wo[j].astype(BF16)
        x = _out_proj(o_p, o_s, wo, x, rows, modarr, 2)

        wqt = peer_wq[l].T.astype(BF16)
        keys = peer_keys[l].reshape(2 * peer_heads, PEER_N_KEYS, PEER_HALF).astype(BF16)
        ht, e1, e0n, tau = _peer_a(x, rows, norm2[l][None, :], modarr, wqt, keys)
        x = _peer_b(ht, u_all, vt_all, l, e1, e0n, tau, x, rows, modarr)

    return (x[:n_p].reshape(bp, sp, d), x[n_p:].reshape(bs, ss, d),
            da_k.reshape(bp, n_da, sp, 2 * da_heads, DA_HEAD_DIM), da_v.reshape(bp, n_da, sp, da_heads, DA_V_DIM),
            mla_c, mla_p)
```

```python
import functools
import math

import jax
import jax.numpy as jnp
from jax import lax
from jax.experimental import pallas as pl
from jax.experimental.pallas import tpu as pltpu

GRID_W = 64
DA_HEAD_DIM = 64
DA_V_DIM = 2 * DA_HEAD_DIM
MLA_NOPE = 64
MLA_ROPE = 32
MLA_QK_DIM = MLA_NOPE + MLA_ROPE
MLA_V_DIM = 64
PEER_N_KEYS = 128
PEER_HALF = 128
PEER_TOPK = 16
ROPE_BASE = 10000.0
EPS = 1e-6

LANE = 128
SUBLANE = 8
VMEM_LIMIT = 56 * 1024 * 1024
TOK_TILE = 256
Q_TILE = 1024
KEY_CHUNK = 1024
LOG2E = math.log2(math.e)
PEER_A_TILE = 256
PEER_B_TILE = 512
PEER_E_TILE = 2048
PEER_B_CHUNK = 256
PEER_OUT_ROWS = 512
PEER_GATE_ROWS = 64
GELU_C0 = math.sqrt(2.0 / math.pi)
GELU_C1 = GELU_C0 * 0.044715

BF16 = jnp.bfloat16
F32 = jnp.float32


def _params(sem):
    return pltpu.CompilerParams(dimension_semantics=sem, vmem_limit_bytes=VMEM_LIMIT)


def _dot(a, b):
    return jnp.dot(a, b, preferred_element_type=F32)


def _dot_nt(a, b):
    return lax.dot_general(a, b, (((1,), (1,)), ((), ())), preferred_element_type=F32)


def _rms(x):
    return x * lax.rsqrt(jnp.mean(x * x, axis=-1, keepdims=True) + EPS)


def _modulate(x, g, shift, scale):
    return _rms(x) * g * (1.0 + scale) + shift


def _rope(x, cos, sin, half):
    lane = lax.broadcasted_iota(jnp.int32, x.shape, 1)
    first = (lane & (2 * half - 1)) < half
    partner = jnp.where(first, pltpu.roll(x, LANE - half, 1), pltpu.roll(x, half, 1))
    return x * cos + partner * sin


def _segments(seg):
    lane = jnp.arange(LANE) // seg
    return (lane[:, None] == lane[None, :]).astype(BF16)


def _head_norm128(t, g, width, seg):
    ms = _dot((t * t).astype(BF16), seg) * (1.0 / width)
    return t * lax.rsqrt(ms + EPS) * g


class _Rows:
    def __init__(self, n_p, n_b, s_lat):
        self.n_p, self.n_b, self.s_lat = n_p, n_b, s_lat
        self.n = n_p + n_b * s_lat

    def mod_row(self, i, tm):
        np_t = self.n_p // tm
        return jnp.where(i < np_t, 0, 1 + (i - np_t) // (self.s_lat // tm))

    def pos_blk(self, i, tm):
        np_t = self.n_p // tm
        return jnp.where(i < np_t, self.s_lat // tm, (i - np_t) % (self.s_lat // tm))


def _mod_spec(rows, tm, which, d):
    return pl.BlockSpec((1, 1, d), lambda i: (rows.mod_row(i, tm) * 6 + which, 0, 0))


def _rope_tables(s_lat, tm, rot_lo, rot_w, period):
    hd = rot_w // 2
    half = hd // 2
    lane = jnp.arange(LANE)
    d = (lane % period) - rot_lo
    rot = (d >= 0) & (d < rot_w)
    d = jnp.clip(d, 0, rot_w - 1)
    use_row = d < hd
    dd = d % hd
    i = dd % half
    first = dd < half
    inv = 1.0 / (ROPE_BASE ** ((2 * i).astype(F32) / hd))
    t = jnp.arange(s_lat)
    row = (t // GRID_W).astype(F32)
    col = (t % GRID_W).astype(F32)
    pos = jnp.where(use_row[None, :], row[:, None], col[:, None])
    ang = pos * inv[None, :]
    cos = jnp.where(rot[None, :], jnp.cos(ang), 1.0)
    sin = jnp.where(rot[None, :], jnp.where(first[None, :], -jnp.sin(ang), jnp.sin(ang)), 0.0)
    cos = jnp.concatenate([cos, jnp.ones((tm, LANE), F32)], axis=0)
    sin = jnp.concatenate([sin, jnp.zeros((tm, LANE), F32)], axis=0)
    return cos.astype(F32), sin.astype(F32)


def _cache_out(prev, shape, blk_tail, rows, tm, layer):
    np_t = rows.n_p // tm
    spec = pl.BlockSpec((None, None) + blk_tail, lambda i: (jnp.minimum(i, np_t - 1), layer, 0, 0))
    sds = jax.ShapeDtypeStruct(shape, F32)
    if prev is None:
        prev = jnp.zeros(shape, F32)
    return spec, sds, [prev], [pl.BlockSpec(memory_space=pl.ANY)]


def _mod_kernel(c_ref, w_ref, b_ref, o_ref):
    c = c_ref[...]
    s = c * jax.nn.sigmoid(c)
    o_ref[...] = _dot(s.astype(BF16), w_ref[...].astype(BF16)) + b_ref[...]


def _modulation(cvecs, ada_w, ada_b):
    depth, d, d6 = ada_w.shape
    r = cvecs.shape[0]
    tn = d
    return pl.pallas_call(
        _mod_kernel,
        grid=(depth, d6 // tn),
        in_specs=[
            pl.BlockSpec((r, d), lambda l, j: (0, 0)),
            pl.BlockSpec((None, d, tn), lambda l, j: (l, 0, j)),
            pl.BlockSpec((None, 1, tn), lambda l, j: (l, 0, j)),
        ],
        out_specs=pl.BlockSpec((None, r, tn), lambda l, j: (l, 0, j)),
        out_shape=jax.ShapeDtypeStruct((depth, r, d6), F32),
        compiler_params=_params(("arbitrary", "arbitrary")),
        name="adaln_modulation",
    )(cvecs, ada_w, ada_b.reshape(depth, 1, d6))


def _da_qkv_kernel(n_prompt_tiles, n_prev, x_ref, n_ref, sh_ref, sc_ref, wq_ref, wk_ref, wv_ref, qn_ref, kn_ref,
                   seg_ref, cos_ref, sin_ref, *refs):
    q_ref, kb_ref, vb_ref, kf_ref, vf_ref = refs[n_prev:]
    h = _modulate(x_ref[...], n_ref[...], sh_ref[0], sc_ref[0]).astype(BF16)
    q = _dot(h, wq_ref[...])
    k = _dot(h, wk_ref[...])
    v = _dot(h, wv_ref[...])
    vb_ref[...] = v.astype(BF16)
    cos = cos_ref[...]
    sin = sin_ref[...]
    seg = seg_ref[...]

    def head_norm(t, g):
        return _head_norm128(t, g, DA_HEAD_DIM, seg)

    scale = DA_HEAD_DIM ** -0.5 * LOG2E
    ks = []
    for j in range(q.shape[1] // LANE):
        sl = slice(j * LANE, (j + 1) * LANE)
        qj = _rope(head_norm(q[:, sl], qn_ref[...]), cos, sin, DA_HEAD_DIM // 4)
        kj = _rope(head_norm(k[:, sl], kn_ref[...]), cos, sin, DA_HEAD_DIM // 4)
        q_ref[:, sl] = (qj * scale).astype(BF16)
        kb_ref[:, sl] = kj.astype(BF16)
        ks.append(kj)

    @pl.when(pl.program_id(0) < n_prompt_tiles)
    def _():
        vf_ref[...] = v
        for j, kj in enumerate(ks):
            kf_ref[:, j * LANE:(j + 1) * LANE] = kj


def _da_qkv(x, rows, norm, modarr, wq, wk, wv, qn, kn, cos, sin, layer, n_layers, prev_k, prev_v, bp, sp):
    n, d = x.shape
    tm = TOK_TILE
    w = wq.shape[1]
    full = lambda shape: pl.BlockSpec(shape, lambda i: (0,) * len(shape))
    tok = lambda width: pl.BlockSpec((tm, width), lambda i: (i, 0))
    tab = pl.BlockSpec((tm, LANE), lambda i: (rows.pos_blk(i, tm), 0))
    k_spec, k_sds, k_in, k_in_specs = _cache_out(prev_k, (bp, n_layers, sp, w), (sp, w), rows, tm, layer)
    v_spec, v_sds, v_in, v_in_specs = _cache_out(prev_v, (bp, n_layers, sp, w), (sp, w), rows, tm, layer)
    n_prev = len(k_in) + len(v_in)
    n_in = 12
    return pl.pallas_call(
        functools.partial(_da_qkv_kernel, rows.n_p // tm, n_prev),
        grid=(n // tm,),
        in_specs=[tok(d), full((1, d)), _mod_spec(rows, tm, 0, d), _mod_spec(rows, tm, 1, d),
                  full((d, w)), full((d, w)), full((d, w)), full((1, LANE)), full((1, LANE)), full((LANE, LANE)),
                  tab, tab] + k_in_specs + v_in_specs,
        out_specs=[tok(w)] * 3 + [k_spec, v_spec],
        out_shape=[jax.ShapeDtypeStruct((n, w), BF16)] * 3 + [k_sds, v_sds],
        input_output_aliases={n_in + a: 3 + a for a in range(n_prev)},
        compiler_params=_params(("arbitrary",)),
        name="da_qkv",
    )(x, norm, modarr, modarr, wq, wk, wv, qn, kn, _segments(DA_HEAD_DIM), cos, sin, *k_in, *v_in)


def _attend(qs, ksls, vsl, kv_refs):
    n = len(qs)
    m, acc = [None] * n, [None] * n
    for k_ref, v_ref in kv_refs:
        sk = k_ref.shape[0]
        step = min(KEY_CHUNK, sk)
        for c0 in range(0, sk, step):
            v = v_ref[c0:c0 + step, vsl].astype(BF16)
            v1 = jnp.concatenate([v, jnp.ones_like(v)], axis=1)
            for a in range(n):
                s = _dot_nt(qs[a], k_ref[c0:c0 + step, ksls[a]].astype(BF16))
                mx = s.max(axis=-1, keepdims=True)
                if m[a] is None:
                    m[a] = mx
                    acc[a] = _dot(jnp.exp2((s - mx).astype(BF16)), v1)
                else:
                    m_new = jnp.maximum(m[a], mx)
                    alpha = jnp.exp2(m[a] - m_new)
                    acc[a] = alpha * acc[a] + _dot(jnp.exp2((s - m_new).astype(BF16)), v1)
                    m[a] = m_new
    return [acc[a][:, :LANE] / acc[a][:, LANE:] for a in range(n)]


def _da_attn_kernel(lam_init, n_chunks, group, lam_ref, subln_ref, q_ref, *refs):
    kv = [(refs[2 * c], refs[2 * c + 1]) for c in range(n_chunks)]
    o_ref = refs[2 * n_chunks]
    lv = lam_ref[...]
    lam = (jnp.exp(jnp.sum(lv[0:1] * lv[1:2], axis=-1, keepdims=True))
           - jnp.exp(jnp.sum(lv[2:3] * lv[3:4], axis=-1, keepdims=True)) + lam_init)
    for g in range(group):
        sl = slice(g * LANE, (g + 1) * LANE)
        q = q_ref[:, sl]
        lane = lax.broadcasted_iota(jnp.int32, q.shape, 1)
        zero = jnp.zeros_like(q)
        qs = [jnp.where(lane < DA_HEAD_DIM, q, zero), jnp.where(lane < DA_HEAD_DIM, zero, q)]
        o1, o2 = _attend(qs, [sl] * 2, sl, kv)
        o = o1 - lam * o2
        o = _rms(o) * subln_ref[...] * (1.0 - lam_init)
        o_ref[:, sl] = o.astype(o_ref.dtype)


def _mla_attn_kernel(n_chunks, group, q_ref, *refs):
    kv = [(refs[2 * c], refs[2 * c + 1]) for c in range(n_chunks)]
    o_ref = refs[2 * n_chunks]
    for g in range(group):
        sls = [slice((2 * g + half) * LANE, (2 * g + half + 1) * LANE) for half in range(2)]
        vsl = slice(g * LANE, (g + 1) * LANE)
        outs = _attend([q_ref[:, sl] for sl in sls], sls, vsl, kv)
        lane = lax.broadcasted_iota(jnp.int32, outs[0].shape, 1)
        o_ref[:, vsl] = jnp.where(lane < MLA_V_DIM, outs[0], outs[1]).astype(o_ref.dtype)


def _attention(kernel, n_pairs, group, qw, q, seq, row_off, n_b, chunks, extra_in, extra_specs, name):
    tq = min(Q_TILE, seq)
    nq = seq // tq
    q_spec = pl.BlockSpec((tq, qw * group), lambda b, hp, qi: (row_off // tq + b * nq + qi, hp))
    in_specs = list(extra_specs) + [q_spec]
    args = list(extra_in) + [q]
    for k, v, sk, off in chunks:
        kv_map = functools.partial(lambda b, hp, qi, o, s: (o // s + b, hp), o=off, s=sk)
        in_specs.append(pl.BlockSpec((sk, qw * group), kv_map))
        in_specs.append(pl.BlockSpec((sk, LANE * group), kv_map))
        args += [k, v]
    return pl.pallas_call(
        functools.partial(kernel, group),
        grid=(n_b, n_pairs // group, nq),
        in_specs=in_specs,
        out_specs=pl.BlockSpec((tq, LANE * group), lambda b, hp, qi: (b * nq + qi, hp)),
        out_shape=jax.ShapeDtypeStruct((n_b * seq, n_pairs * LANE), BF16),
        compiler_params=_params(("arbitrary", "arbitrary", "arbitrary")),
        name=name,
    )(*args)


def _mla_q_kernel(kv_lora, n_prompt_tiles, n_prev, x_ref, n_ref, sh_ref, sc_ref, wdq_ref, qan_ref, wuq_ref, qn_ref,
                  wdkv_ref, kvn_ref, seg_ref, cos_ref, sin_ref, *refs):
    q_ref, ckv_ref, kpe_ref, ckvp_ref, kpep_ref = refs[n_prev:]
    h = _modulate(x_ref[...], n_ref[...], sh_ref[0], sc_ref[0]).astype(BF16)
    cq = (_rms(_dot(h, wdq_ref[...])) * qan_ref[...]).astype(BF16)
    q = _dot(cq, wuq_ref[...])
    cos = cos_ref[...]
    sin = sin_ref[...]
    scale = MLA_QK_DIM ** -0.5 * LOG2E
    for j in range(q.shape[1] // LANE):
        sl = slice(j * LANE, (j + 1) * LANE)
        qj = _rope(_head_norm128(q[:, sl], qn_ref[...], MLA_QK_DIM, seg_ref[...]), cos, sin, MLA_ROPE // 4)
        q_ref[:, sl] = (qj * scale).astype(BF16)
    ckvf = _dot(h, wdkv_ref[...])
    ckv = _rms(ckvf[:, :kv_lora]) * kvn_ref[...]
    ckv_ref[...] = ckv
    kpe_ref[...] = ckvf[:, kv_lora:]

    @pl.when(pl.program_id(0) < n_prompt_tiles)
    def _():
        ckvp_ref[...] = ckv
        kpep_ref[...] = ckvf[:, kv_lora + MLA_NOPE:kv_lora + MLA_QK_DIM]


def _mla_q(x, rows, norm, modarr, wdq, qan, wuq, qn, wdkv, kvn, cos, sin, layer, n_layers, prev_c, prev_p, bp, sp):
    n, d = x.shape
    tm = TOK_TILE
    ql = wdq.shape[1]
    qw = wuq.shape[1]
    kv_lora = kvn.shape[1]
    full = lambda shape: pl.BlockSpec(shape, lambda i: (0,) * len(shape))
    tok = lambda width: pl.BlockSpec((tm, width), lambda i: (i, 0))
    tab = pl.BlockSpec((tm, LANE), lambda i: (rows.pos_blk(i, tm), 0))
    c_spec, c_sds, c_in, c_in_specs = _cache_out(prev_c, (bp, n_layers, sp, kv_lora), (sp, kv_lora), rows, tm, layer)
    p_spec, p_sds, p_in, p_in_specs = _cache_out(prev_p, (bp, n_layers, sp, MLA_ROPE), (sp, MLA_ROPE), rows, tm, layer)
    n_prev = len(c_in) + len(p_in)
    n_in = 13
    return pl.pallas_call(
        functools.partial(_mla_q_kernel, kv_lora, rows.n_p // tm, n_prev),
        grid=(n // tm,),
        in_specs=[tok(d), full((1, d)), _mod_spec(rows, tm, 0, d), _mod_spec(rows, tm, 1, d),
                  full((d, ql)), full((1, ql)), full((ql, qw)), full((1, LANE)),
                  full((d, kv_lora + LANE)), full((1, kv_lora)), full((LANE, LANE)), tab, tab]
        + c_in_specs + p_in_specs,
        out_specs=[tok(qw), tok(kv_lora), tok(LANE), c_spec, p_spec],
        out_shape=[jax.ShapeDtypeStruct((n, qw), BF16), jax.ShapeDtypeStruct((n, kv_lora), F32),
                   jax.ShapeDtypeStruct((n, LANE), F32), c_sds, p_sds],
        input_output_aliases={n_in + a: 3 + a for a in range(n_prev)},
        compiler_params=_params(("arbitrary",)),
        name="mla_q_ckv",
    )(x, norm, modarr, modarr, wdq, qan, wuq, qn, wdkv, kvn, _segments(LANE), cos, sin, *c_in, *p_in)


def _mla_kv_kernel(ckv_ref, kpe_ref, wuk_ref, wuv_ref, kn_ref, seg_ref, cos_ref, sin_ref, k_ref, v_ref):
    c = ckv_ref[...].astype(BF16)
    kn = _dot(c, wuk_ref[...])
    v_ref[...] = _dot(c, wuv_ref[...]).astype(BF16)
    kpe = kpe_ref[...]
    cos = cos_ref[...]
    sin = sin_ref[...]
    for j in range(kn.shape[1] // LANE):
        sl = slice(j * LANE, (j + 1) * LANE)
        kj = _rope(_head_norm128(kn[:, sl] + kpe, kn_ref[...], MLA_QK_DIM, seg_ref[...]), cos, sin, MLA_ROPE // 4)
        k_ref[:, sl] = kj.astype(BF16)


def _mla_kv(ckv, kpe, wuk, wuv, kn, cos, sin, pos_blk):
    n, kv_lora = ckv.shape
    tm = TOK_TILE
    kw = wuk.shape[1]
    vw = wuv.shape[1]
    full = lambda shape: pl.BlockSpec(shape, lambda i: (0,) * len(shape))
    tok = lambda width: pl.BlockSpec((tm, width), lambda i: (i, 0))
    tab = pl.BlockSpec((tm, LANE), lambda i: (pos_blk(i), 0))
    return pl.pallas_call(
        _mla_kv_kernel,
        grid=(n // tm,),
        in_specs=[tok(kv_lora), tok(LANE), full((kv_lora, kw)), full((kv_lora, vw)), full((1, LANE)),
                  full((LANE, LANE)), tab, tab],
        out_specs=[tok(kw), tok(vw)],
        out_shape=[jax.ShapeDtypeStruct((n, kw), BF16), jax.ShapeDtypeStruct((n, vw), BF16)],
        compiler_params=_params(("arbitrary",)),
        name="mla_kv_expand",
    )(ckv, kpe, wuk, wuv, kn, _segments(LANE), cos, sin)


def _sort_pairs(n):
    pairs = []
    p = 1
    while p < n:
        k = p
        while k >= 1:
            for j in range(k % p, n - k, 2 * k):
                for i in range(min(k, n - j - k)):
                    if (i + j) // (2 * p) == (i + j + k) // (2 * p):
                        pairs.append((i + j, i + j + k))
            k //= 2
        p *= 2
    return pairs


def _cmpx(x, i, j):
    a, b = x[i], x[j]
    if b is None:
        return
    if a is None:
        x[i], x[j] = b, None
        return
    x[i], x[j] = jnp.maximum(a, b), jnp.minimum(a, b)


def _top_sorted(x):
    k = PEER_TOPK
    x = list(x)
    for i, j in _sort_pairs(k):
        _cmpx(x, i, j)
    shift = SUBLANE // 2
    while shift >= 1:
        y = []
        for i in range(k):
            other = x[k - 1 - i]
            other = None if other is None else pltpu.roll(other, SUBLANE - shift, 0)
            if x[i] is None:
                y.append(other)
            elif other is None:
                y.append(x[i])
            else:
                y.append(jnp.maximum(x[i], other))
        d = k // 2
        while d >= 1:
            for i in range(k):
                if (i & d) == 0:
                    _cmpx(y, i, i + d)
            d //= 2
        x = y
        shift //= 2
    return x


def _pair_candidates(a_ref, b_ref, scale):
    k = PEER_TOPK
    b_lo = b_ref[0:8, :]
    sub = lax.broadcasted_iota(jnp.int32, b_lo.shape, 0)
    row = lambda r: a_ref[r:r + 1, :] * scale
    out = [row(0) * b_lo, row(0) * b_ref[8:16, :]]
    for r in range(1, 8):
        n_r = k // (r + 1)
        c = row(r) * b_lo
        out.append(c if n_r >= 8 else jnp.where(sub < n_r, c, -1.0))
    out.append((a_ref[8:16, :] * scale) * b_ref[0:1, :])
    return out + [None] * (k - len(out))


def _peer_a_kernel(n_heads, n_prompt_tiles, op_ref, os_ref, wo_ref, x_ref, g_ref, n_ref, sh_ref, sc_ref, wqt_ref,
                   keys_ref, x1_ref, ht_ref, e1_ref, e0n_ref, tau_ref, qt_sc, e_sc, top_sc):
    k = PEER_TOPK
    o = jnp.where(pl.program_id(0) < n_prompt_tiles, op_ref[...], os_ref[...])
    x1 = x_ref[...] + g_ref[0] * _dot(o, wo_ref[...])
    x1_ref[...] = x1
    h = _modulate(x1, n_ref[...], sh_ref[0], sc_ref[0])
    ht = h.T.astype(BF16)
    ht_ref[...] = ht
    qt_sc[...] = _dot(wqt_ref[...], ht)

    for hc in range(2 * n_heads):
        qb = qt_sc[hc * PEER_HALF:(hc + 1) * PEER_HALF, :]
        s = _dot(keys_ref[hc], qb.astype(BF16))
        e_sc[hc] = jnp.exp(s - jnp.max(s, axis=0, keepdims=True))

    def per_set(hc, carry):
        e = e_sc[hc]
        top = _top_sorted([e[g * SUBLANE:(g + 1) * SUBLANE, :] for g in range(PEER_N_KEYS // SUBLANE)])
        for r in range(k):
            top_sc[hc, r:r + 1, :] = top[r][0:1, :]
        return carry

    lax.fori_loop(0, 2 * n_heads, per_set, 0)

    def per_head(hh, carry):
        a_ref = top_sc.at[2 * hh]
        b_ref = top_sc.at[2 * hh + 1]
        cands = _pair_candidates(a_ref, b_ref, 1.0)
        top = _top_sorted(cands)
        z = top[0][0:1, :]
        for r in range(1, k):
            z = z + top[r][0:1, :]
        rz = 0.5 / z
        e0n_ref[hh] = e_sc[2 * hh] * rz
        e1_ref[hh] = e_sc[2 * hh + 1]
        kth = top[k - 1][0:1, :]
        taun = None
        for cu, cn in zip(cands, _pair_candidates(a_ref, b_ref, rz)):
            if cu is not None:
                v = jnp.where(cu == kth, cn, jnp.inf)
                taun = v if taun is None else jnp.minimum(taun, v)
        tau_ref[pl.ds(hh, 1), :] = jnp.min(taun, axis=0, keepdims=True)
        return carry

    lax.fori_loop(0, n_heads, per_head, 0)


def _peer_a(o_p, o_s, wo, x, rows, norm, modarr, wqt, keys):
    n, d = x.shape
    t = PEER_A_TILE
    kw = o_p.shape[1]
    np_t = rows.n_p // t
    n_sets = keys.shape[0]
    n_heads = n_sets // 2
    full = lambda shape: pl.BlockSpec(shape, lambda i: (0,) * len(shape))
    return pl.pallas_call(
        functools.partial(_peer_a_kernel, n_heads, np_t),
        grid=(n // t,),
        in_specs=[pl.BlockSpec((t, kw), lambda i: (jnp.minimum(i, np_t - 1), 0)),
                  pl.BlockSpec((t, kw), lambda i: (jnp.maximum(i - np_t, 0), 0)),
                  full((kw, d)), pl.BlockSpec((t, d), lambda i: (i, 0)), _mod_spec(rows, t, 2, d), full((1, d)),
                  _mod_spec(rows, t, 3, d), _mod_spec(rows, t, 4, d),
                  full(wqt.shape), full(keys.shape)],
        out_specs=[pl.BlockSpec((t, d), lambda i: (i, 0)),
                   pl.BlockSpec((d, t), lambda i: (0, i)),
                   pl.BlockSpec((n_heads, PEER_N_KEYS, t), lambda i: (0, 0, i)),
                   pl.BlockSpec((n_heads, PEER_N_KEYS, t), lambda i: (0, 0, i)),
                   pl.BlockSpec((n_heads, t), lambda i: (0, i))],
        out_shape=[jax.ShapeDtypeStruct((n, d), F32),
                   jax.ShapeDtypeStruct((d, n), BF16),
                   jax.ShapeDtypeStruct((n_heads, PEER_N_KEYS, n), F32),
                   jax.ShapeDtypeStruct((n_heads, PEER_N_KEYS, n), F32),
                   jax.ShapeDtypeStruct((n_heads, n), F32)],
        scratch_shapes=[pltpu.VMEM((n_sets * PEER_HALF, t), F32),
                        pltpu.VMEM((n_sets, PEER_N_KEYS, t), F32),
                        pltpu.VMEM((n_sets, PEER_TOPK, t), F32)],
        compiler_params=_params(("arbitrary",)),
        name="peer_select",
    )(o_p, o_s, wo, x, modarr, norm, modarr, modarr, wqt, keys)


def _peer_b_kernel(n_heads, n_chunks, ht_ref, u_ref, vt_ref, e1_ref, e0n_ref, tau_ref, x_ref, g_ref, y_ref,
                   acc_sc, *chunk_sc):
    at_sc, w_sc = chunk_sc[:n_chunks], chunk_sc[n_chunks:]
    e = pl.program_id(1)

    @pl.when(e == 0)
    def _():
        acc_sc[...] = jnp.zeros_like(acc_sc)

    ht = ht_ref[...]
    t = ht.shape[1]
    ch = PEER_B_CHUNK
    for c in range(n_chunks):
        at_sc[c][...] = _dot(u_ref[c * ch:(c + 1) * ch, :], ht)
    for c in range(n_chunks):
        for r0 in range(0, ch, PEER_GATE_ROWS):
            r1 = r0 + PEER_GATE_ROWS
            ii, j0 = divmod(c * ch + r0, PEER_N_KEYS)
            for l0 in range(0, t, LANE):
                ls = slice(l0, l0 + LANE)
                gate = None
                for hh in range(n_heads):
                    p = e1_ref[hh, j0:j0 + PEER_GATE_ROWS, ls] * e0n_ref[hh, ii:ii + 1, ls]
                    g = jnp.where(p >= tau_ref[hh:hh + 1, ls], p, 0.0)
                    gate = g if gate is None else gate + g
                a = at_sc[c][r0:r1, ls]
                th = jnp.tanh(a * (GELU_C0 + GELU_C1 * (a * a)))
                w_sc[c][r0:r1, ls] = (gate * (a + a * th)).astype(BF16)
        wc = w_sc[c][...]
        for d0 in range(0, acc_sc.shape[0], PEER_OUT_ROWS):
            d1 = min(d0 + PEER_OUT_ROWS, acc_sc.shape[0])
            acc_sc[d0:d1, :] += _dot(vt_ref[d0:d1, c * ch:(c + 1) * ch], wc)

    @pl.when(e == pl.num_programs(1) - 1)
    def _():
        y_ref[...] = x_ref[...] + g_ref[0] * acc_sc[...].T


def _peer_b(ht, u, vt, layer, e1, e0n, tau, x, rows, modarr):
    n, d = x.shape
    t = PEER_B_TILE
    te = PEER_E_TILE
    n_exp = u.shape[1]
    n_heads = e1.shape[0]
    ti = te // PEER_N_KEYS
    n_chunks = te // PEER_B_CHUNK
    return pl.pallas_call(
        functools.partial(_peer_b_kernel, n_heads, n_chunks),
        grid=(n // t, n_exp // te),
        in_specs=[pl.BlockSpec((d, t), lambda i, e: (0, i)),
                  pl.BlockSpec((None, te, d), lambda i, e: (layer, e, 0)),
                  pl.BlockSpec((None, d, te), lambda i, e: (layer, 0, e)),
                  pl.BlockSpec((n_heads, PEER_N_KEYS, t), lambda i, e: (0, 0, i)),
                  pl.BlockSpec((n_heads, ti, t), lambda i, e: (0, e, i)),
                  pl.BlockSpec((n_heads, t), lambda i, e: (0, i)),
                  pl.BlockSpec((t, d), lambda i, e: (i, 0)),
                  pl.BlockSpec((1, 1, d), lambda i, e: (rows.mod_row(i, t) * 6 + 5, 0, 0))],
        out_specs=pl.BlockSpec((t, d), lambda i, e: (i, 0)),
        out_shape=jax.ShapeDtypeStruct((n, d), F32),
        scratch_shapes=([pltpu.VMEM((d, t), F32)] + [pltpu.VMEM((PEER_B_CHUNK, t), F32)] * n_chunks
                        + [pltpu.VMEM((PEER_B_CHUNK, t), BF16)] * n_chunks),
        compiler_params=_params(("arbitrary", "arbitrary")),
        name="peer_experts",
    )(ht, u, vt, e1, e0n, tau, x, modarr)


def _pad_heads(w, n_heads, width):
    k = w.shape[0]
    w = w.reshape(k, n_heads, width)
    return jnp.pad(w, ((0, 0), (0, 0), (0, LANE - width))).reshape(k, n_heads * LANE)


def _pad_lanes(g, lo=0):
    g = g.reshape(1, -1).astype(F32)
    return jnp.pad(g, ((0, 0), (lo, LANE - lo - g.shape[1])))


def kernel(x_prompt, x_sample, cache_da_k, cache_da_v, cache_mla_ckv, cache_mla_kpe, c, c_ctx, norm1, norm2, ada_w, ada_b, da_wq, da_wk, da_wv, da_wo, da_q_norm, da_k_norm, da_lq1, da_lk1, da_lq2, da_lk2, da_subln, mla_wdq, mla_qa_norm, mla_wuq, mla_wdkv, mla_kv_norm, mla_wukv, mla_q_norm, mla_k_norm, mla_wo, peer_wq, peer_keys, peer_u, peer_v):
    bp, sp, d = x_prompt.shape
    bs, ss, _ = x_sample.shape
    depth = norm1.shape[0]
    past = cache_da_k.shape[2]
    n_p = bp * sp
    rows = _Rows(n_p, bs, ss)
    tm = TOK_TILE
    assert sp % tm == 0 and ss % tm == 0 and n_p % ss == 0 and past == sp
    assert n_p % PEER_B_TILE == 0 and ss % PEER_B_TILE == 0

    da_heads = da_wv.shape[2] // DA_V_DIM
    mla_heads = mla_wo.shape[1] // MLA_V_DIM
    kv_lora = mla_kv_norm.shape[1]
    peer_heads = peer_keys.shape[1]

    x = jnp.concatenate([x_prompt.reshape(n_p, d), x_sample.reshape(bs * ss, d)], axis=0)

    n_cond = 1 + bs
    r_pad = -(-n_cond // 8) * 8
    cvecs = jnp.concatenate([c_ctx[None, :], c, jnp.zeros((r_pad - n_cond, d), F32)], axis=0)
    mods = _modulation(cvecs, ada_w, ada_b).reshape(depth, r_pad * 6, 1, d)

    da_cos, da_sin = _rope_tables(ss, tm, 0, DA_HEAD_DIM, DA_HEAD_DIM)
    mla_cos, mla_sin = _rope_tables(ss, tm, MLA_NOPE, MLA_ROPE, LANE)
    ident = ss // tm

    assert tm == sp
    n_da = (depth + 1) // 2
    n_mla = depth // 2
    da_k = da_v = mla_c = mla_p = None
    u_all = peer_u.astype(BF16)
    vt_all = jnp.swapaxes(peer_v, 1, 2).astype(BF16)
    for l in range(depth):
        modarr = mods[l]
        j = l // 2
        if l % 2 == 0:
            lam_init = 0.8 - 0.6 * math.exp(-0.3 * l)
            q, kb, vb, da_k, da_v = _da_qkv(
                x, rows, norm1[l][None, :], modarr,
                da_wq[j].astype(BF16), da_wk[j].astype(BF16), da_wv[j].astype(BF16),
                jnp.tile(da_q_norm[j][None, :], (1, 2)), jnp.tile(da_k_norm[j][None, :], (1, 2)), da_cos, da_sin,
                j, n_da, da_k, da_v, bp, sp)
            lam_vecs = jnp.stack([da_lq1[j], da_lk1[j], da_lq2[j], da_lk2[j]]).astype(F32)
            extra = [lam_vecs, da_subln[j][None, :]]
            extra_specs = [pl.BlockSpec(lam_vecs.shape, lambda b, hp, qi: (0, 0)),
                           pl.BlockSpec((1, DA_V_DIM), lambda b, hp, qi: (0, 0))]
            ck = cache_da_k[:, j].reshape(bs * past, 2 * da_heads * DA_HEAD_DIM)
            cv = cache_da_v[:, j].reshape(bs * past, da_heads * DA_V_DIM)
            o_p = _attention(functools.partial(_da_attn_kernel, lam_init, 1), da_heads, da_heads, LANE, q, sp, 0, bp,
                             [(kb, vb, sp, 0)], extra, extra_specs, "da_attn_prompt")
            o_s = _attention(functools.partial(_da_attn_kernel, lam_init, 2), da_heads, 1, LANE, q, ss, n_p, bs,
                             [(kb, vb, ss, n_p), (ck, cv, past, 0)], extra, extra_specs, "da_attn_latent")
            wo = da_wo[j].astype(BF16)
        else:
            wuq = _pad_heads(mla_wuq[j], mla_heads, MLA_QK_DIM).astype(BF16)
            wdkv = jnp.concatenate([
                mla_wdkv[j][:, :kv_lora],
                jnp.pad(mla_wdkv[j][:, kv_lora:], ((0, 0), (MLA_NOPE, LANE - MLA_QK_DIM)))], axis=1).astype(BF16)
            wukv = mla_wukv[j].reshape(kv_lora, mla_heads, MLA_NOPE + MLA_V_DIM)
            wuk = jnp.pad(wukv[:, :, :MLA_NOPE], ((0, 0), (0, 0), (0, LANE - MLA_NOPE)))
            wuk = wuk.reshape(kv_lora, mla_heads * LANE).astype(BF16)
            wuv = wukv[:, :, MLA_NOPE:].reshape(kv_lora, mla_heads * MLA_V_DIM).astype(BF16)
            q, ckv, kpe, mla_c, mla_p = _mla_q(
                x, rows, norm1[l][None, :], modarr, mla_wdq[j].astype(BF16), mla_qa_norm[j][None, :], wuq,
                _pad_lanes(mla_q_norm[j]), wdkv, mla_kv_norm[j][None, :], mla_cos, mla_sin,
                j, n_mla, mla_c, mla_p, bp, sp)
            kn = _pad_lanes(mla_k_norm[j])
            k, v = _mla_kv(ckv, kpe, wuk, wuv, kn, mla_cos, mla_sin, lambda i: rows.pos_blk(i, tm))
            c_ckv = cache_mla_ckv[:, j].reshape(bs * past, kv_lora)
            c_kpe = jnp.pad(cache_mla_kpe[:, j].reshape(bs * past, MLA_ROPE), ((0, 0), (MLA_NOPE, LANE - MLA_QK_DIM)))
            ck, cv = _mla_kv(c_ckv, c_kpe, wuk, wuv, kn, mla_cos, mla_sin, lambda i: ident)
            o_p = _attention(functools.partial(_mla_attn_kernel, 1), mla_heads // 2, mla_heads // 2, 2 * LANE, q, sp, 0, bp,
                             [(k, v, sp, 0)], [], [], "mla_attn_prompt")
            o_s = _attention(functools.partial(_mla_attn_kernel, 2), mla_heads // 2, 1, 2 * LANE, q, ss, n_p, bs,
                             [(k, v, ss, n_p), (ck, cv, past, 0)], [], [], "mla_attn_latent")
            wo = mla_wo[j].astype(BF16)

        wqt = peer_wq[l].T.astype(BF16)
        keys = peer_keys[l].reshape(2 * peer_heads, PEER_N_KEYS, PEER_HALF).astype(BF16)
        x, ht, e1, e0n, tau = _peer_a(o_p, o_s, wo, x, rows, norm2[l][None, :], modarr, wqt, keys)
        x = _peer_b(ht, u_all, vt_all, l, e1, e0n, tau, x, rows, modarr)

    return (x[:n_p].reshape(bp, sp, d), x[n_p:].reshape(bs, ss, d),
            da_k.reshape(bp, n_da, sp, 2 * da_heads, DA_HEAD_DIM), da_v.reshape(bp, n_da, sp, da_heads, DA_V_DIM),
            mla_c, mla_p)
```
